```python
import jax, jax.numpy as jnp
from jax import lax
import numpy as np

D_MODEL = 1024
BATCH = 8
SEQ = 2048
DEPTH = 1
DEC_BATCH = 32
DEC_SEQ = 1
PAST_LEN = 8192
PAGE_SIZE = 128

N_HEADS = 16
HEAD_DIM = 64
ATTN_WIDTH = N_HEADS * HEAD_DIM
CONV_CH = 1024
CONV_WIDTH = 31
N_EXPERTS = 32
TOP_K = 4
D_FF = 1024
SWIGLU_ALPHA = 1.702
SWIGLU_LIMIT = 7.0
Q_BLOCK = 128
LN_EPS = 1e-5
DEEPNORM_ALPHA = (2 * DEPTH) ** 0.25
DEEPNORM_BETA = (8 * DEPTH) ** -0.25
D_IN_PROJ = 3 * ATTN_WIDTH + N_HEADS + 2 * CONV_CH + 2 * D_MODEL

kernel_name = "fox_conformer_gated_moe_step"


def layer_norm(x, g, b):
    xf = x.astype(jnp.float32)
    mu = jnp.mean(xf, axis=-1, keepdims=True)
    var = jnp.mean(jnp.square(xf - mu), axis=-1, keepdims=True)
    y = (xf - mu) * lax.rsqrt(var + LN_EPS) * g.astype(jnp.float32) + b.astype(jnp.float32)
    return y.astype(x.dtype)


def in_projection(h, w_in, b_forget):
    N, L, _ = h.shape
    sizes = [ATTN_WIDTH, ATTN_WIDTH, ATTN_WIDTH, N_HEADS, CONV_CH, CONV_CH, D_MODEL]
    points = [int(p) for p in np.cumsum(sizes)]
    proj = h @ w_in
    q, k, v, f, u_val, u_gate, g_attn, g_conv = jnp.split(proj, points, axis=-1)
    q = q.reshape(N, L, N_HEADS, HEAD_DIM)
    k = k.reshape(N, L, N_HEADS, HEAD_DIM)
    v = v.reshape(N, L, N_HEADS, HEAD_DIM)
    logf = jax.nn.log_sigmoid(f.astype(jnp.float32) + b_forget.astype(jnp.float32))
    u = u_val * jax.nn.sigmoid(u_gate)
    return q, k, v, logf, u, g_attn, g_conv


def forgetting_attention_prompt(q, k, v, logf):
    B, S = q.shape[0], q.shape[1]
    n_blk = S // Q_BLOCK
    scale = HEAD_DIM ** -0.5
    c = jnp.cumsum(logf, axis=1)
    c_keys = jnp.transpose(c, (0, 2, 1))
    k_pos = jnp.arange(S)

    def one_block(args):
        q_blk, c_blk, q_pos = args
        s = jnp.einsum('bqhd,bkhd->bhqk', q_blk, k).astype(jnp.float32) * scale
        bias = jnp.transpose(c_blk, (0, 2, 1))[..., None] - c_keys[:, :, None, :]
        mask = k_pos[None, :] <= q_pos[:, None]
        p = jax.nn.softmax(jnp.where(mask, s + bias, -jnp.inf), axis=-1).astype(v.dtype)
        return jnp.einsum('bhqk,bkhd->bqhd', p, v)

    q_blocks = q.reshape(B, n_blk, Q_BLOCK, N_HEADS, HEAD_DIM).transpose(1, 0, 2, 3, 4)
    c_blocks = c.reshape(B, n_blk, Q_BLOCK, N_HEADS).transpose(1, 0, 2, 3)
    pos_blocks = jnp.arange(S).reshape(n_blk, Q_BLOCK)
    out = lax.map(one_block, (q_blocks, c_blocks, pos_blocks))
    return out.transpose(1, 0, 2, 3, 4).reshape(B, S, N_HEADS, HEAD_DIM)


def forgetting_attention_sample(q, k_new, v_new, logf_new, cache_k, cache_v, cache_logf, page_table):
    DB, DS = q.shape[0], q.shape[1]
    past = page_table.shape[1] * PAGE_SIZE
    scale = HEAD_DIM ** -0.5
    k_past = cache_k[page_table].reshape(DB, past, N_HEADS, HEAD_DIM)
    v_past = cache_v[page_table].reshape(DB, past, N_HEADS, HEAD_DIM)
    lf_past = cache_logf[page_table].reshape(DB, past, N_HEADS).astype(jnp.float32)
    k_all = jnp.concatenate([k_past, k_new.astype(k_past.dtype)], axis=1)
    v_all = jnp.concatenate([v_past, v_new.astype(v_past.dtype)], axis=1)
    lf_all = jnp.concatenate([lf_past, logf_new], axis=1)
    after = lax.cumsum(lf_all, axis=1, reverse=True) - lf_all
    after_k = jnp.transpose(after, (0, 2, 1))
    bias = after_k[:, :, None, :] - after_k[:, :, past:, None]
    s = jnp.einsum('bqhd,bkhd->bhqk', q, k_all).astype(jnp.float32) * scale + bias
    q_pos = past + jnp.arange(DS)
    k_pos = jnp.arange(past + DS)
    mask = k_pos[None, :] <= q_pos[:, None]
    p = jax.nn.softmax(jnp.where(mask, s, -jnp.inf), axis=-1).astype(v_all.dtype)
    return jnp.einsum('bhqk,bkhd->bqhd', p, v_all)


def causal_depthwise_conv(u, prev, conv_w, conv_b):
    full = jnp.concatenate([prev.astype(u.dtype), u], axis=1)
    y = lax.conv_general_dilated(full, conv_w[:, None, :].astype(u.dtype), window_strides=(1,),
                                 padding='VALID', dimension_numbers=('NWC', 'WIO', 'NWC'),
                                 feature_group_count=CONV_CH)
    return y + conv_b, full[:, -(CONV_WIDTH - 1):]


def conv_branch(u, prev, conv_w, conv_b, conv_norm_g, conv_norm_b):
    y, new_prev = causal_depthwise_conv(u, prev, conv_w, conv_b)
    y = layer_norm(y, conv_norm_g, conv_norm_b)
    return jax.nn.silu(y), new_prev


def merge_branches(att, conv, g_attn, g_conv, w_attn_proj, w_conv_proj, w_out):
    N, L = att.shape[0], att.shape[1]
    a = att.reshape(N, L, ATTN_WIDTH) @ w_attn_proj
    c = conv @ w_conv_proj
    return (jax.nn.sigmoid(g_attn) * a + jax.nn.sigmoid(g_conv) * c) @ w_out


def routed_experts(x, w_router, b_router, w_gate_up, b_gate_up, w_down, b_down):
    logits = (x @ w_router).astype(jnp.float32) + b_router.astype(jnp.float32)
    top_logits, top_idx = lax.top_k(logits, TOP_K)
    top_w = jax.nn.softmax(top_logits, axis=-1)
    combine = jnp.einsum('tk,tke->te', top_w,
                         jax.nn.one_hot(top_idx, N_EXPERTS, dtype=jnp.float32)).astype(x.dtype)
    out = jnp.zeros_like(x)
    for e in range(N_EXPERTS):
        gu = x @ w_gate_up[e] + b_gate_up[e]
        gate = jnp.minimum(gu[:, :D_FF], SWIGLU_LIMIT)
        up = jnp.clip(gu[:, D_FF:], -SWIGLU_LIMIT, SWIGLU_LIMIT)
        act = (up + 1.0) * gate * jax.nn.sigmoid(SWIGLU_ALPHA * gate)
        out = out + combine[:, e:e + 1] * (act @ w_down[e] + b_down[e])
    return out


def trunk_layer(xp, xs, cache_k, cache_v, cache_logf, state_conv, page_table,
                w_in, b_forget, conv_w, conv_b, conv_norm_g, conv_norm_b,
                w_attn_proj, w_conv_proj, w_out, ln1_g, ln1_b,
                w_router, b_router, w_gate_up, b_gate_up, w_down, b_down, ln2_g, ln2_b):
    B, S, D = xp.shape
    DB, DS, _ = xs.shape
    qp, kp, vp, lfp, up, gap, gcp = in_projection(xp, w_in, b_forget)
    att_p = forgetting_attention_prompt(qp, kp, vp, lfp)
    conv_p, conv_state_p = conv_branch(up, jnp.zeros((B, CONV_WIDTH - 1, CONV_CH), up.dtype),
                                       conv_w, conv_b, conv_norm_g, conv_norm_b)
    mix_p = merge_branches(att_p, conv_p, gap, gcp, w_attn_proj, w_conv_proj, w_out)
    qs, ks, vs, lfs, us, gas, gcs = in_projection(xs, w_in, b_forget)
    att_s = forgetting_attention_sample(qs, ks, vs, lfs, cache_k, cache_v, cache_logf, page_table)
    conv_s, conv_state_s = conv_branch(us, state_conv, conv_w, conv_b, conv_norm_g, conv_norm_b)
    mix_s = merge_branches(att_s, conv_s, gas, gcs, w_attn_proj, w_conv_proj, w_out)
    res = jnp.concatenate([(DEEPNORM_ALPHA * xp + mix_p).reshape(B * S, D),
                           (DEEPNORM_ALPHA * xs + mix_s).reshape(DB * DS, D)], axis=0)
    h = layer_norm(res, ln1_g, ln1_b)
    h = layer_norm(DEEPNORM_ALPHA * h + routed_experts(h, w_router, b_router, w_gate_up,
                                                        b_gate_up, w_down, b_down), ln2_g, ln2_b)
    y_p = h[:B * S].reshape(B, S, D)
    y_s = h[B * S:].reshape(DB, DS, D)
    return (y_p, y_s, kp, vp, lfp, conv_state_p, ks, vs, lfs, conv_state_s)


def setup_inputs(seed: int = 0) -> dict:
    key = jax.random.key(seed)
    ks = jax.random.split(key, 32)
    f32 = jnp.float32
    n_pages = PAST_LEN // PAGE_SIZE
    n_phys = (5 * DEC_BATCH * n_pages + 3) // 4

    def normal(k, shape, scale=1.0):
        return jax.random.normal(k, shape, f32) * scale

    x_prompt = normal(ks[0], (BATCH, SEQ, D_MODEL))
    x_sample = normal(ks[1], (DEC_BATCH, DEC_SEQ, D_MODEL))
    cache_k = normal(ks[2], (DEPTH, n_phys, PAGE_SIZE, N_HEADS, HEAD_DIM))
    cache_v = normal(ks[3], (DEPTH, n_phys, PAGE_SIZE, N_HEADS, HEAD_DIM))
    cache_logf = jax.nn.log_sigmoid(
        jax.random.uniform(ks[4], (DEPTH, 1, 1, N_HEADS), f32, 1.0, 6.0)
        + normal(ks[5], (DEPTH, n_phys, PAGE_SIZE, N_HEADS)))
    state_conv = normal(ks[6], (DEPTH, DEC_BATCH, CONV_WIDTH - 1, CONV_CH), 0.5)
    page_table = jax.random.permutation(ks[7], n_phys)[:DEC_BATCH * n_pages].reshape(
        DEC_BATCH, n_pages).astype(jnp.int32)
    return {
        "x_prompt": x_prompt,
        "x_sample": x_sample,
        "cache_k": cache_k,
        "cache_v": cache_v,
        "cache_logf": cache_logf,
        "state_conv": state_conv,
        "page_table": page_table,
        "w_in": normal(ks[8], (DEPTH, D_MODEL, D_IN_PROJ), D_MODEL ** -0.5),
        "b_forget": jax.random.uniform(ks[9], (DEPTH, N_HEADS), f32, 1.0, 6.0),
        "conv_w": normal(ks[10], (DEPTH, CONV_WIDTH, CONV_CH), CONV_WIDTH ** -0.5),
        "conv_b": normal(ks[11], (DEPTH, CONV_CH), 0.02),
        "conv_norm_g": 1.0 + normal(ks[12], (DEPTH, CONV_CH), 0.02),
        "conv_norm_b": normal(ks[13], (DEPTH, CONV_CH), 0.02),
        "w_attn_proj": normal(ks[14], (DEPTH, ATTN_WIDTH, D_MODEL), ATTN_WIDTH ** -0.5),
        "w_conv_proj": normal(ks[15], (DEPTH, CONV_CH, D_MODEL), CONV_CH ** -0.5),
        "w_out": normal(ks[16], (DEPTH, D_MODEL, D_MODEL), D_MODEL ** -0.5 * DEEPNORM_BETA),
        "ln1_g": 1.0 + normal(ks[17], (DEPTH, D_MODEL), 0.02),
        "ln1_b": normal(ks[18], (DEPTH, D_MODEL), 0.02),
        "w_router": normal(ks[19], (DEPTH, D_MODEL, N_EXPERTS), D_MODEL ** -0.5),
        "b_router": normal(ks[20], (DEPTH, N_EXPERTS), 0.01),
        "w_gate_up": normal(ks[21], (DEPTH, N_EXPERTS, D_MODEL, 2 * D_FF), D_MODEL ** -0.5),
        "b_gate_up": normal(ks[22], (DEPTH, N_EXPERTS, 2 * D_FF), 0.02),
        "w_down": normal(ks[23], (DEPTH, N_EXPERTS, D_FF, D_MODEL), D_FF ** -0.5 * DEEPNORM_BETA),
        "b_down": normal(ks[24], (DEPTH, N_EXPERTS, D_MODEL), 0.02),
        "ln2_g": 1.0 + normal(ks[25], (DEPTH, D_MODEL), 0.02),
        "ln2_b": normal(ks[26], (DEPTH, D_MODEL), 0.02),
    }


def reference(x_prompt, x_sample, cache_k, cache_v, cache_logf, state_conv, page_table,
              w_in, b_forget, conv_w, conv_b, conv_norm_g, conv_norm_b,
              w_attn_proj, w_conv_proj, w_out, ln1_g, ln1_b,
              w_router, b_router, w_gate_up, b_gate_up, w_down, b_down, ln2_g, ln2_b):
    xp, xs = x_prompt, x_sample
    rows = []
    for layer in range(DEPTH):
        out = trunk_layer(xp, xs, cache_k[layer], cache_v[layer], cache_logf[layer],
                          state_conv[layer], page_table,
                          w_in[layer], b_forget[layer], conv_w[layer], conv_b[layer],
                          conv_norm_g[layer], conv_norm_b[layer], w_attn_proj[layer],
                          w_conv_proj[layer], w_out[layer], ln1_g[layer], ln1_b[layer],
                          w_router[layer], b_router[layer], w_gate_up[layer], b_gate_up[layer],
                          w_down[layer], b_down[layer], ln2_g[layer], ln2_b[layer])
        xp, xs = out[0], out[1]
        rows.append(out[2:])
    k_prompt = jnp.stack([r[0] for r in rows])
    v_prompt = jnp.stack([r[1] for r in rows])
    logf_prompt = jnp.stack([r[2] for r in rows])
    conv_prompt = jnp.stack([r[3] for r in rows])
    k_sample = jnp.stack([r[4] for r in rows])
    v_sample = jnp.stack([r[5] for r in rows])
    logf_sample = jnp.stack([r[6] for r in rows])
    conv_sample = jnp.stack([r[7] for r in rows])
    return (xp, xs, k_prompt, v_prompt, logf_prompt, conv_prompt,
            k_sample, v_sample, logf_sample, conv_sample)
```

```python
import functools

import jax
import jax.numpy as jnp
from jax import lax
from jax.experimental import pallas as pl
from jax.experimental.pallas import tpu as pltpu

F32 = jnp.float32
BF16 = jnp.bfloat16

D_MODEL = 1024
N_HEADS = 16
HEAD_DIM = 64
CONV_WIDTH = 31
N_EXPERTS = 32
TOP_K = 4
D_FF = 1024
PAGE_SIZE = 128
SWIGLU_ALPHA = 1.702
SWIGLU_LIMIT = 7.0
LN_EPS = 1e-5
DEPTH = 1
DEEPNORM_ALPHA = (2 * DEPTH) ** 0.25

LANES = 128
HEADS_PER_LANE_TILE = LANES // HEAD_DIM
VMEM_LIMIT = 56 * 2 ** 20

TOKEN_TILE = 256
ATTN_TILE = 256
ROW_TILE = 256
PAGES_PER_STEP = 8
CUMSUM_BLOCK = 256


def _params(*sem):
    return pltpu.CompilerParams(dimension_semantics=sem, vmem_limit_bytes=VMEM_LIMIT)


def _split3(x):
    hi = x.astype(BF16)
    r = x - hi.astype(F32)
    mid = r.astype(BF16)
    lo = (r - mid.astype(F32)).astype(BF16)
    return hi, mid, lo


def _dot(a, b):
    return jnp.dot(a, b, preferred_element_type=F32)


def _dot_exact_rhs(x, rhs_bf16):
    hi, mid, lo = _split3(x)
    return _dot(hi, rhs_bf16) + _dot(mid, rhs_bf16) + _dot(lo, rhs_bf16)


def _dot_exact_lhs(lhs_bf16, x):
    hi, mid, lo = _split3(x)
    return _dot(lhs_bf16, hi) + _dot(lhs_bf16, mid) + _dot(lhs_bf16, lo)


def _log_sigmoid(x):
    return jnp.minimum(x, 0.0) - jnp.log1p(jnp.exp(-jnp.abs(x)))


def _sigmoid(x):
    return 1.0 / (1.0 + jnp.exp(-x))


def _layer_norm(x, g, b):
    mu = jnp.mean(x, axis=-1, keepdims=True)
    xc = x - mu
    var = jnp.mean(xc * xc, axis=-1, keepdims=True)
    return xc * lax.rsqrt(var + LN_EPS) * g + b


def _inproj_kernel(x_ref, w_ref, wf_ref, wft_ref, bf_ref, bft_ref,
                   q_ref, k_ref, v_ref, kb_ref, vb_ref, lf_ref, lft_ref, u_ref, ga_ref, gc_ref):
    x = x_ref[...].astype(BF16)
    q = _dot(x, w_ref[0])
    q_ref[...] = (q * (HEAD_DIM ** -0.5)).astype(BF16)
    k = _dot(x, w_ref[1])
    k_ref[...] = k
    kb_ref[...] = k.astype(BF16)
    v = _dot(x, w_ref[2])
    v_ref[...] = v
    vb_ref[...] = v.astype(BF16)
    u_ref[...] = _dot(x, w_ref[3]) * _sigmoid(_dot(x, w_ref[4]))
    ga_ref[...] = _sigmoid(_dot(x, w_ref[5]))
    gc_ref[...] = _sigmoid(_dot(x, w_ref[6]))
    f = _dot(x, wf_ref[...])
    lf_ref[...] = _log_sigmoid(f + bf_ref[...])
    ft = lax.dot_general(wft_ref[...], x, (((1,), (1,)), ((), ())),
                         preferred_element_type=F32)
    lft_ref[...] = _log_sigmoid(ft + bft_ref[...])


def _in_projection(x, w_main, w_f, w_ft, b_f, b_ft, tm):
    n = x.shape[0]
    grid = (n // tm,)
    row = lambda i: (i, 0)
    const2 = lambda i: (0, 0)
    big = lambda dt: jax.ShapeDtypeStruct((n, D_MODEL), dt)
    tile = pl.BlockSpec((tm, D_MODEL), row)
    out_shape = (big(BF16), big(F32), big(F32), big(BF16), big(BF16),
                 jax.ShapeDtypeStruct((n, N_HEADS), F32),
                 jax.ShapeDtypeStruct((N_HEADS, n), F32),
                 big(F32), big(F32), big(F32))
    out_specs = (tile, tile, tile, tile, tile,
                 pl.BlockSpec((tm, N_HEADS), row),
                 pl.BlockSpec((N_HEADS, tm), lambda i: (0, i)),
                 tile, tile, tile)
    in_specs = [tile,
                pl.BlockSpec(w_main.shape, lambda i: (0, 0, 0), pipeline_mode=pl.Buffered(1)),
                pl.BlockSpec(w_f.shape, const2),
                pl.BlockSpec(w_ft.shape, const2),
                pl.BlockSpec(b_f.shape, const2),
                pl.BlockSpec(b_ft.shape, const2)]
    return pl.pallas_call(_inproj_kernel, grid=grid, in_specs=in_specs, out_specs=out_specs,
                          out_shape=out_shape, compiler_params=_params("parallel"),
                          name="in_projection")(x, w_main, w_f, w_ft, b_f, b_ft)


def _cumsum_kernel(lft_ref, c_ref):
    s = lft_ref.shape[1]
    r = lax.broadcasted_iota(jnp.int32, (CUMSUM_BLOCK, CUMSUM_BLOCK), 0)
    c = lax.broadcasted_iota(jnp.int32, (CUMSUM_BLOCK, CUMSUM_BLOCK), 1)
    upper = jnp.where(r <= c, 1.0, 0.0).astype(BF16)
    carry = jnp.zeros((N_HEADS, 1), F32)
    for blk in range(s // CUMSUM_BLOCK):
        sl = slice(blk * CUMSUM_BLOCK, (blk + 1) * CUMSUM_BLOCK)
        cs = _dot_exact_rhs(lft_ref[:, sl], upper) + carry
        c_ref[:, sl] = cs
        carry = cs[:, CUMSUM_BLOCK - 1:CUMSUM_BLOCK]


def _forget_cumsum(lft, batch, seq):
    spec = pl.BlockSpec((N_HEADS, seq), lambda b: (0, b))
    return pl.pallas_call(_cumsum_kernel, grid=(batch,), in_specs=[spec], out_specs=spec,
                          out_shape=jax.ShapeDtypeStruct(lft.shape, F32),
                          compiler_params=_params("parallel"), name="forget_cumsum")(lft)


def _flash_kernel(q_ref, k_ref, v_ref, c_ref, o_ref):
    t = ATTN_TILE
    i = pl.program_id(2)
    lane = lax.broadcasted_iota(jnp.int32, (1, LANES), 1)
    first = lane < HEAD_DIM
    q2 = q_ref[...]
    zero = jnp.zeros_like(q2)
    qh = (jnp.where(first, q2, zero), jnp.where(first, zero, q2))
    q_start = pl.multiple_of(i * t, t)
    c_blk = c_ref[0, :, pl.ds(q_start, t)]
    c_ref0 = c_blk[:, 0:1]
    nt = (((1,), (1,)), ((), ()))

    def step(j, carry, masked):
        ks = pl.multiple_of(j * t, t)
        k2 = k_ref[pl.ds(ks, t), :]
        v2 = v_ref[pl.ds(ks, t), :]
        vzero = jnp.zeros_like(v2)
        vh = (jnp.where(first, v2, vzero), jnp.where(first, vzero, v2))
        bias = c_ref0 - c_ref[0, :, pl.ds(ks, t)]
        new = []
        pv = None
        scale_lanes = None
        for h in range(HEADS_PER_LANE_TILE):
            m, l = carry[2 * h], carry[2 * h + 1]
            s = lax.dot_general(qh[h], k2, nt, preferred_element_type=F32) + bias[h:h + 1, :]
            if masked:
                rr = lax.broadcasted_iota(jnp.int32, (t, t), 0)
                cc = lax.broadcasted_iota(jnp.int32, (t, t), 1)
                s = jnp.where(cc <= rr, s, -jnp.inf)
            m_new = jnp.maximum(m, jnp.max(s, axis=-1, keepdims=True))
            alpha = jnp.exp(m - m_new)
            p = jnp.exp(s - m_new)
            l_new = alpha * l + jnp.sum(p, axis=-1, keepdims=True)
            contrib = _dot(p.astype(BF16), vh[h])
            pv = contrib if pv is None else pv + contrib
            new += [m_new, l_new]
            a_l = jnp.broadcast_to(alpha, (t, LANES))
            scale_lanes = a_l if scale_lanes is None else jnp.where(first, scale_lanes, a_l)
        acc = carry[4] * scale_lanes + pv
        return (new[0], new[1], new[2], new[3], acc)

    neg = jnp.full((t, 1), -jnp.inf, F32)
    zl = jnp.zeros((t, 1), F32)
    init = (neg, zl, neg, zl, jnp.zeros((t, LANES), F32))
    carry = lax.fori_loop(0, i, lambda j, c: step(j, c, False), init)
    m0, l0, m1, l1, acc = step(i, carry, True)
    inv = jnp.where(first, jnp.broadcast_to(1.0 / l0, (t, LANES)), jnp.broadcast_to(1.0 / l1, (t, LANES)))
    o_ref[...] = (acc * inv).astype(o_ref.dtype)


def _prompt_attention(qb, kb, vb, c, batch, seq):
    t = ATTN_TILE
    nq = seq // t
    pairs = N_HEADS // HEADS_PER_LANE_TILE
    q_spec = pl.BlockSpec((t, LANES), lambda b, hp, i: (b * nq + i, hp))
    kv_spec = pl.BlockSpec((seq, LANES), lambda b, hp, i: (b, hp))
    c_spec = pl.BlockSpec((1, HEADS_PER_LANE_TILE, seq), lambda b, hp, i: (hp, 0, b))
    return pl.pallas_call(_flash_kernel, grid=(batch, pairs, nq),
                          in_specs=[q_spec, kv_spec, kv_spec, c_spec], out_specs=q_spec,
                          out_shape=jax.ShapeDtypeStruct(qb.shape, BF16),
                          compiler_params=_params("parallel", "parallel", "arbitrary"),
                          name="prompt_attention")(qb, kb, vb, c)


def _paged_kernel(pt_ref, *refs):
    pp = PAGES_PER_STEP
    k_refs, v_refs, lf_refs = refs[:pp], refs[pp:2 * pp], refs[2 * pp:3 * pp]
    q_ref, kn_ref, vn_ref, lfn_ref, o_ref, m_ref, l_ref, c_ref, acc_ref = refs[3 * pp:]
    g = pl.program_id(1)
    width = N_HEADS * HEAD_DIM

    @pl.when(g == 0)
    def _():
        m_ref[...] = jnp.full(m_ref.shape, -jnp.inf, F32)
        l_ref[...] = jnp.zeros(l_ref.shape, F32)
        c_ref[...] = jnp.zeros(c_ref.shape, F32)
        acc_ref[...] = jnp.zeros(acc_ref.shape, F32)

    e_row = lax.broadcasted_iota(jnp.int32, (N_HEADS, width), 0)
    e_col = lax.broadcasted_iota(jnp.int32, (N_HEADS, width), 1) // HEAD_DIM
    own = e_row == e_col
    expand = jnp.where(own, 1.0, 0.0).astype(BF16)
    q_rows = jnp.where(own, jnp.broadcast_to(q_ref[0].astype(F32), (N_HEADS, width)),
                       0.0).astype(BF16)
    nt = (((1,), (1,)), ((), ()))
    rr = lax.broadcasted_iota(jnp.int32, (PAGE_SIZE, PAGE_SIZE), 0)
    cc = lax.broadcasted_iota(jnp.int32, (PAGE_SIZE, PAGE_SIZE), 1)
    lower = jnp.where(cc <= rr, 1.0, 0.0).astype(BF16)

    def absorb(logit, values, rows):
        m = m_ref[...]
        m_new = jnp.maximum(m, jnp.max(logit, axis=0, keepdims=True))
        alpha = jnp.exp(m - m_new)
        p = jnp.exp(logit - m_new)
        l_ref[...] = alpha * l_ref[...] + jnp.sum(p, axis=0, keepdims=True)
        m_ref[...] = m_new
        p_wide = _dot(p.astype(BF16), expand)
        a_wide = _dot_exact_rhs(jnp.broadcast_to(alpha, (8, N_HEADS)), expand)
        weighted = (p_wide * values).reshape(rows // 8, 8, width)
        acc_ref[...] = acc_ref[...] * a_wide + jnp.sum(weighted, axis=0)

    for p_i in range(pp):
        lf = lf_refs[p_i][0]
        incl = _dot_exact_lhs(lower, lf) + c_ref[...]
        c_ref[...] = incl[PAGE_SIZE - 1:PAGE_SIZE, :]
        s = lax.dot_general(k_refs[p_i][0].astype(BF16), q_rows, nt,
                            preferred_element_type=F32)
        absorb(s - incl, v_refs[p_i][0], PAGE_SIZE)

    @pl.when(g == pl.num_programs(1) - 1)
    def _():
        kn = jnp.broadcast_to(kn_ref[0], (8, width)).astype(BF16)
        s_new = lax.dot_general(kn, q_rows, nt, preferred_element_type=F32)
        incl_new = c_ref[...] + lfn_ref[0]
        logit = s_new - incl_new
        m = m_ref[...]
        m_new = jnp.maximum(m, logit[0:1, :])
        alpha = jnp.exp(m - m_new)
        p = jnp.exp(logit - m_new)
        l_fin = alpha * l_ref[...] + p[0:1, :]
        p_wide = _dot(p.astype(BF16), expand)
        a_wide = _dot_exact_rhs(jnp.broadcast_to(alpha, (8, N_HEADS)), expand)
        vn = vn_ref[0].astype(BF16).astype(F32)
        total = jnp.sum(acc_ref[...] * a_wide, axis=0, keepdims=True) + p_wide[0:1, :] * vn
        l_wide = _dot_exact_rhs(jnp.broadcast_to(l_fin, (8, N_HEADS)), expand)
        o_ref[0] = (total / l_wide[0:1, :]).astype(o_ref.dtype)


def _sample_attention(q_s, k_s, v_s, lf_s, cache_k, cache_v, cache_logf, page_table):
    n_req, n_pages = page_table.shape
    pp = PAGES_PER_STEP
    steps = n_pages // pp
    width = N_HEADS * HEAD_DIM
    n_phys = cache_k.shape[0]
    ck = cache_k.reshape(n_phys, PAGE_SIZE, width)
    cv = cache_v.reshape(n_phys, PAGE_SIZE, width)

    def page_map(p_i):
        return lambda r, g, pt: (pt[r * n_pages + g * pp + p_i], 0, 0)

    k_specs = [pl.BlockSpec((1, PAGE_SIZE, width), page_map(p)) for p in range(pp)]
    lf_specs = [pl.BlockSpec((1, PAGE_SIZE, N_HEADS), page_map(p)) for p in range(pp)]
    row3 = lambda r, g, pt: (r, 0, 0)
    wide = pl.BlockSpec((1, 1, width), row3)
    narrow = pl.BlockSpec((1, 1, N_HEADS), row3)
    grid_spec = pltpu.PrefetchScalarGridSpec(
        num_scalar_prefetch=1, grid=(n_req, steps),
        in_specs=k_specs + k_specs + lf_specs + [wide, wide, wide, narrow],
        out_specs=wide,
        scratch_shapes=[pltpu.VMEM((1, N_HEADS), F32), pltpu.VMEM((1, N_HEADS), F32),
                        pltpu.VMEM((1, N_HEADS), F32), pltpu.VMEM((8, width), F32)])
    args = [ck] * pp + [cv] * pp + [cache_logf] * pp
    args += [q_s.reshape(n_req, 1, width), k_s.reshape(n_req, 1, width),
             v_s.reshape(n_req, 1, width), lf_s.reshape(n_req, 1, N_HEADS)]
    out = pl.pallas_call(_paged_kernel, grid_spec=grid_spec,
                         out_shape=jax.ShapeDtypeStruct((n_req, 1, width), BF16),
                         compiler_params=_params("parallel", "arbitrary"),
                         name="sample_attention")(page_table.reshape(-1), *args)
    return out.reshape(n_req, width)


def _conv_prompt_kernel(u_ref, w_ref, b_ref, g_ref, bn_ref, o_ref, buf_ref, y_ref):
    ts = u_ref.shape[0]
    halo = 32
    j = pl.program_id(1)

    @pl.when(j == 0)
    def _():
        buf_ref[0:halo, :] = jnp.zeros((halo, D_MODEL), F32)

    buf_ref[halo:halo + ts, :] = u_ref[...]
    for c in range(D_MODEL // LANES):
        sl = slice(c * LANES, (c + 1) * LANES)
        acc = jnp.zeros((ts, LANES), F32)
        for tap in range(CONV_WIDTH):
            off = halo - (CONV_WIDTH - 1) + tap
            acc = acc + buf_ref[off:off + ts, sl] * w_ref[tap:tap + 1, sl]
        y_ref[:, sl] = acc + b_ref[:, sl]
    buf_ref[0:halo, :] = buf_ref[ts:ts + halo, :]
    y = _layer_norm(y_ref[...], g_ref[...], bn_ref[...])
    o_ref[...] = (y * _sigmoid(y)).astype(o_ref.dtype)


def _conv_prompt(u, conv_w, conv_b, g, bn, batch, seq):
    ts = TOKEN_TILE
    ns = seq // ts
    tile = pl.BlockSpec((ts, D_MODEL), lambda b, j: (b * ns + j, 0))
    const = lambda a: pl.BlockSpec(a.shape, lambda b, j: (0, 0))
    return pl.pallas_call(_conv_prompt_kernel, grid=(batch, ns),
                          in_specs=[tile, const(conv_w), const(conv_b), const(g), const(bn)],
                          out_specs=tile, out_shape=jax.ShapeDtypeStruct(u.shape, BF16),
                          scratch_shapes=[pltpu.VMEM((32 + ts, D_MODEL), F32),
                                          pltpu.VMEM((ts, D_MODEL), F32)],
                          compiler_params=_params("parallel", "arbitrary"),
                          name="conv_prompt")(u, conv_w, conv_b, g, bn)


def _conv_sample_kernel(state_ref, u_ref, w_ref, b_ref, g_ref, bn_ref, o_ref):
    hist = CONV_WIDTH - 1
    y = jnp.sum(state_ref[...] * w_ref[0:hist, :][None], axis=1)
    y = y + u_ref[...] * w_ref[hist:hist + 1, :] + b_ref[...]
    y = _layer_norm(y, g_ref[...], bn_ref[...])
    o_ref[...] = (y * _sigmoid(y)).astype(o_ref.dtype)


def _conv_sample(state, u, conv_w, conv_b, g, bn):
    full = lambda a: pl.BlockSpec(a.shape, lambda i: (0,) * a.ndim)
    args = (state, u, conv_w, conv_b, g, bn)
    return pl.pallas_call(_conv_sample_kernel, grid=(1,), in_specs=[full(a) for a in args],
                          out_specs=full(u), out_shape=jax.ShapeDtypeStruct(u.shape, BF16),
                          compiler_params=_params("arbitrary"), name="conv_sample")(*args)


def _merge_router_kernel(att_ref, conv_ref, ga_ref, gc_ref, x_ref, wa_ref, wc_ref, wo_ref,
                         g1_ref, b1_ref, wr_ref, br_ref, cnt_in_ref,
                         h_ref, idx_ref, wts_ref, rank_ref, cnt_ref):
    tm = x_ref.shape[0]

    @pl.when(pl.program_id(0) == 0)
    def _():
        cnt_ref[...] = cnt_in_ref[...]

    a = _dot(att_ref[...], wa_ref[...])
    c = _dot(conv_ref[...], wc_ref[...])
    mixed = (ga_ref[...] * a + gc_ref[...] * c).astype(BF16)
    res = DEEPNORM_ALPHA * x_ref[...] + _dot(mixed, wo_ref[...])
    h = _layer_norm(res, g1_ref[...], b1_ref[...])
    h_ref[...] = h

    hh, hm, hl = _split3(h)
    w_hi, w_mid, w_lo = wr_ref[0], wr_ref[1], wr_ref[2]
    logits = (_dot(hh, w_hi) + _dot(hh, w_mid) + _dot(hm, w_hi)
              + _dot(hh, w_lo) + _dot(hm, w_mid) + _dot(hl, w_hi)) + br_ref[...]

    eid = lax.broadcasted_iota(jnp.int32, (tm, N_EXPERTS), 1).astype(F32)
    k_lane = lax.broadcasted_iota(jnp.int32, (tm, TOP_K), 1)
    remaining = logits
    chosen = jnp.zeros((tm, N_EXPERTS), F32)
    vals, picks = [], []
    for _ in range(TOP_K):
        mx = jnp.max(remaining, axis=-1, keepdims=True)
        pick = jnp.min(jnp.where(remaining == mx, eid, N_EXPERTS), axis=-1, keepdims=True)
        hit = eid == pick
        chosen = jnp.where(hit, 1.0, chosen)
        remaining = jnp.where(hit, -jnp.inf, remaining)
        vals.append(mx)
        picks.append(pick)
    exps = [jnp.exp(v - vals[0]) for v in vals]
    denom = exps[0] + exps[1] + exps[2] + exps[3]

    rr = lax.broadcasted_iota(jnp.int32, (tm, tm), 0)
    cc = lax.broadcasted_iota(jnp.int32, (tm, tm), 1)
    strict_lower = jnp.where(cc < rr, 1.0, 0.0).astype(BF16)
    rank_dense = _dot(strict_lower, chosen.astype(BF16)) + cnt_ref[...]

    idx_out = jnp.zeros((tm, TOP_K), F32)
    wts_out = jnp.zeros((tm, TOP_K), F32)
    rank_out = jnp.zeros((tm, TOP_K), F32)
    for k in range(TOP_K):
        rk = jnp.sum(jnp.where(eid == picks[k], rank_dense, 0.0), axis=-1, keepdims=True)
        idx_out = jnp.where(k_lane == k, picks[k], idx_out)
        wts_out = jnp.where(k_lane == k, exps[k] / denom, wts_out)
        rank_out = jnp.where(k_lane == k, rk, rank_out)
    idx_ref[...] = idx_out.astype(jnp.int32)
    wts_ref[...] = wts_out
    rank_ref[...] = rank_out.astype(jnp.int32)
    cnt_ref[...] = cnt_ref[...] + jnp.sum(chosen, axis=0, keepdims=True)


def _merge_router(att, conv, ga, gc, x, wa, wc, wo, g1, b1, wr3, br, cnt_in, tm):
    n = x.shape[0]
    row = lambda i: (i, 0)
    tile = pl.BlockSpec((tm, D_MODEL), row)
    small = pl.BlockSpec((tm, TOP_K), row)
    const = lambda a: pl.BlockSpec(a.shape, lambda i: (0,) * a.ndim)
    out_shape = (jax.ShapeDtypeStruct((n, D_MODEL), F32),
                 jax.ShapeDtypeStruct((n, TOP_K), jnp.int32),
                 jax.ShapeDtypeStruct((n, TOP_K), F32),
                 jax.ShapeDtypeStruct((n, TOP_K), jnp.int32),
                 jax.ShapeDtypeStruct((1, N_EXPERTS), F32))
    return pl.pallas_call(
        _merge_router_kernel, grid=(n // tm,),
        in_specs=[tile, tile, tile, tile, tile, const(wa), const(wc), const(wo),
                  const(g1), const(b1), const(wr3), const(br), const(cnt_in)],
        out_specs=(tile, small, small, small, const(cnt_in)), out_shape=out_shape,
        compiler_params=_params("arbitrary"), name="merge_router",
    )(att, conv, ga, gc, x, wa, wc, wo, g1, b1, wr3, br, cnt_in)


def _row_copy(src_ref, src_row, dst_ref, dst_row, sem):
    return pltpu.make_async_copy(src_ref.at[pl.ds(src_row, 1), :], dst_ref.at[pl.ds(dst_row, 1), :], sem)


def _dispatch_kernel(pos_ref, h_ref, xs_in_ref, xs_ref, sem):
    del xs_in_ref
    tm = h_ref.shape[0]

    def issue(t, carry):
        for k in range(TOP_K):
            _row_copy(h_ref, t, xs_ref, pos_ref[0, 0, t * TOP_K + k], sem).start()
        return carry

    lax.fori_loop(0, tm, issue, 0)

    def drain(t, carry):
        for k in range(TOP_K):
            _row_copy(h_ref, 0, xs_ref, 0, sem).wait()
        return carry

    lax.fori_loop(0, tm, drain, 0)


def _dispatch(pos, h, xs, tm):
    n = h.shape[0]
    pos3 = pos.reshape(n // tm, 1, tm * TOP_K)
    return pl.pallas_call(
        _dispatch_kernel, grid=(n // tm,),
        in_specs=[pl.BlockSpec((1, 1, tm * TOP_K), lambda i: (i, 0, 0), memory_space=pltpu.SMEM),
                  pl.BlockSpec((tm, D_MODEL), lambda i: (i, 0)),
                  pl.BlockSpec(memory_space=pl.ANY)],
        out_specs=pl.BlockSpec(memory_space=pl.ANY),
        out_shape=jax.ShapeDtypeStruct(xs.shape, xs.dtype),
        scratch_shapes=[pltpu.SemaphoreType.DMA(())],
        input_output_aliases={2: 0},
        compiler_params=_params("arbitrary"), name="moe_dispatch")(pos3, h, xs)


def _experts_kernel(te_ref, tv_ref, xs_ref, wgu_ref, bgu_ref, wd_ref, bd_ref, o_ref, wgu_bf, wd_bf):
    n = pl.program_id(0)
    prev = te_ref[jnp.maximum(n - 1, 0)]

    @pl.when((n == 0) | (te_ref[n] != prev))
    def _():
        wgu_bf[...] = wgu_ref[0].astype(BF16)
        wd_bf[...] = wd_ref[0].astype(BF16)

    @pl.when(tv_ref[n] == 1)
    def _():
        gu = _dot(xs_ref[...].astype(BF16), wgu_bf[...]) + bgu_ref[0]
        gate = jnp.minimum(gu[:, :D_FF], SWIGLU_LIMIT)
        up = jnp.clip(gu[:, D_FF:], -SWIGLU_LIMIT, SWIGLU_LIMIT)
        act = (up + 1.0) * gate * _sigmoid(SWIGLU_ALPHA * gate)
        o_ref[...] = _dot(act.astype(BF16), wd_bf[...]) + bd_ref[0]

    @pl.when(tv_ref[n] == 0)
    def _():
        o_ref[...] = jnp.zeros(o_ref.shape, F32)


def _experts(tile_expert, tile_valid, xs, w_gate_up, b_gate_up, w_down, b_down):
    rows = xs.shape[0]
    tm = ROW_TILE
    e3 = lambda n, te, tv: (te[n], 0, 0)
    row = lambda n, te, tv: (n, 0)
    grid_spec = pltpu.PrefetchScalarGridSpec(
        num_scalar_prefetch=2, grid=(rows // tm,),
        in_specs=[pl.BlockSpec((tm, D_MODEL), row),
                  pl.BlockSpec((1, D_MODEL, 2 * D_FF), e3),
                  pl.BlockSpec((1, 1, 2 * D_FF), e3),
                  pl.BlockSpec((1, D_FF, D_MODEL), e3),
                  pl.BlockSpec((1, 1, D_MODEL), e3)],
        out_specs=pl.BlockSpec((tm, D_MODEL), row),
        scratch_shapes=[pltpu.VMEM((D_MODEL, 2 * D_FF), BF16), pltpu.VMEM((D_FF, D_MODEL), BF16)])
    return pl.pallas_call(_experts_kernel, grid_spec=grid_spec,
                          out_shape=jax.ShapeDtypeStruct((rows, D_MODEL), F32),
                          compiler_params=_params("arbitrary"), name="moe_experts",
                          )(tile_expert, tile_valid, xs, w_gate_up,
                            b_gate_up.reshape(N_EXPERTS, 1, 2 * D_FF), w_down,
                            b_down.reshape(N_EXPERTS, 1, D_MODEL))


def _combine_kernel(pos_ref, wts_ref, h_ref, ys_ref, g_ref, b_ref, o_ref, buf_ref, sem):
    tm = h_ref.shape[0]

    def issue(t, carry):
        for k in range(TOP_K):
            _row_copy(ys_ref, pos_ref[0, 0, t * TOP_K + k], buf_ref.at[k], t, sem).start()
        return carry

    lax.fori_loop(0, tm, issue, 0)

    def drain(t, carry):
        for k in range(TOP_K):
            _row_copy(ys_ref, 0, buf_ref.at[k], 0, sem).wait()
        return carry

    lax.fori_loop(0, tm, drain, 0)
    wts = wts_ref[...]
    moe = wts[:, 0:1] * buf_ref[0]
    for k in range(1, TOP_K):
        moe = moe + wts[:, k:k + 1] * buf_ref[k]
    o_ref[...] = _layer_norm(DEEPNORM_ALPHA * h_ref[...] + moe, g_ref[...], b_ref[...])


def _combine(pos, wts, h, ys, g2, b2, tm):
    n = h.shape[0]
    pos3 = pos.reshape(n // tm, 1, tm * TOP_K)
    row = lambda i: (i, 0)
    const = lambda a: pl.BlockSpec(a.shape, lambda i: (0,) * a.ndim)
    return pl.pallas_call(
        _combine_kernel, grid=(n // tm,),
        in_specs=[pl.BlockSpec((1, 1, tm * TOP_K), lambda i: (i, 0, 0), memory_space=pltpu.SMEM),
                  pl.BlockSpec((tm, TOP_K), row),
                  pl.BlockSpec((tm, D_MODEL), row),
                  pl.BlockSpec(memory_space=pl.ANY), const(g2), const(b2)],
        out_specs=pl.BlockSpec((tm, D_MODEL), row),
        out_shape=jax.ShapeDtypeStruct((n, D_MODEL), F32),
        scratch_shapes=[pltpu.VMEM((TOP_K, tm, D_MODEL), F32), pltpu.SemaphoreType.DMA(())],
        compiler_params=_params("arbitrary"), name="moe_combine")(pos3, wts, h, ys, g2, b2)


def _split_in_proj(w_in, b_forget):
    a = N_HEADS * HEAD_DIM
    cuts = [0, a, 2 * a, 3 * a]
    f0 = 3 * a
    rest = f0 + N_HEADS
    starts = cuts[:3] + [rest + i * D_MODEL for i in range(4)]
    w_main = jnp.stack([w_in[:, s:s + D_MODEL] for s in starts]).astype(BF16)
    w_f = w_in[:, f0:rest].astype(BF16)
    return w_main, w_f, w_f.T, b_forget.reshape(1, N_HEADS), b_forget.reshape(N_HEADS, 1)


def _routing_tables(counts, n_tiles):
    cnt = counts.reshape(N_EXPERTS).astype(jnp.int32)
    tiles = (cnt + ROW_TILE - 1) // ROW_TILE
    tile_end = jnp.cumsum(tiles)
    start_row = (tile_end - tiles) * ROW_TILE
    n = jnp.arange(n_tiles, dtype=jnp.int32)
    valid = n < tile_end[-1]
    owner = jnp.sum((n[:, None] >= tile_end[None, :]).astype(jnp.int32), axis=1)
    last_owner = jnp.sum((tile_end[-1] - 1 >= tile_end).astype(jnp.int32))
    tile_expert = jnp.where(valid, owner, last_owner).astype(jnp.int32)
    return start_row, tile_expert, valid.astype(jnp.int32)


def _layer(xp, xs, cache_k, cache_v, cache_logf, state_conv, page_table,
           w_in, b_forget, conv_w, conv_b, conv_norm_g, conv_norm_b,
           w_attn_proj, w_conv_proj, w_out, ln1_g, ln1_b,
           w_router, b_router, w_gate_up, b_gate_up, w_down, b_down, ln2_g, ln2_b):
    batch, seq, d = xp.shape
    n_req = xs.shape[0]
    n_p = batch * seq
    row = lambda a: a.reshape(1, -1)

    w_main, w_f, w_ft, b_f, b_ft = _split_in_proj(w_in, b_forget)
    wa, wc, wo = (w.astype(BF16) for w in (w_attn_proj, w_conv_proj, w_out))
    wr3 = jnp.stack(_split3(w_router))
    cw, cb, cg, cbn = conv_w, row(conv_b), row(conv_norm_g), row(conv_norm_b)
    g1, b1, g2, b2, br = row(ln1_g), row(ln1_b), row(ln2_g), row(ln2_b), row(b_router)

    xp2 = xp.reshape(n_p, d)
    qb, k_p, v_p, kb, vb, lf_p, lft_p, u_p, ga_p, gc_p = _in_projection(
        xp2, w_main, w_f, w_ft, b_f, b_ft, TOKEN_TILE)
    c = _forget_cumsum(lft_p, batch, seq).reshape(N_HEADS // HEADS_PER_LANE_TILE,
                                                  HEADS_PER_LANE_TILE, n_p)
    att_p = _prompt_attention(qb, kb, vb, c, batch, seq)
    conv_p = _conv_prompt(u_p, cw, cb, cg, cbn, batch, seq)

    xs2 = xs.reshape(n_req, d)
    q_s, k_s, v_s, _, _, lf_s, _, u_s, ga_s, gc_s = _in_projection(
        xs2, w_main, w_f, w_ft, b_f, b_ft, n_req)
    att_s = _sample_attention(q_s, k_s, v_s, lf_s, cache_k, cache_v, cache_logf, page_table)
    conv_s = _conv_sample(state_conv, u_s, cw, cb, cg, cbn)

    zero_cnt = jnp.zeros((1, N_EXPERTS), F32)
    h_p, idx_p, wts_p, rank_p, cnt_p = _merge_router(
        att_p, conv_p, ga_p, gc_p, xp2, wa, wc, wo, g1, b1, wr3, br, zero_cnt, TOKEN_TILE)
    h_s, idx_s, wts_s, rank_s, cnt = _merge_router(
        att_s, conv_s, ga_s, gc_s, xs2, wa, wc, wo, g1, b1, wr3, br, cnt_p, n_req)

    n_tok = n_p + n_req
    n_tiles = (n_tok * TOP_K + N_EXPERTS * (ROW_TILE - 1) + ROW_TILE - 1) // ROW_TILE
    start_row, tile_expert, tile_valid = _routing_tables(cnt, n_tiles)
    pos_p = start_row[idx_p] + rank_p
    pos_s = start_row[idx_s] + rank_s
    sorted_rows = jnp.zeros((n_tiles * ROW_TILE, d), F32)
    sorted_rows = _dispatch(pos_p, h_p, sorted_rows, TOKEN_TILE)
    sorted_rows = _dispatch(pos_s, h_s, sorted_rows, n_req)
    expert_out = _experts(tile_expert, tile_valid, sorted_rows, w_gate_up, b_gate_up, w_down, b_down)
    y_p = _combine(pos_p, wts_p, h_p, expert_out, g2, b2, TOKEN_TILE)
    y_s = _combine(pos_s, wts_s, h_s, expert_out, g2, b2, n_req)

    hist = CONV_WIDTH - 1
    conv_state_p = u_p.reshape(batch, seq, d)[:, seq - hist:, :]
    conv_state_s = jnp.concatenate([state_conv[:, 1:, :], u_s[:, None, :]], axis=1)
    return (y_p.reshape(batch, seq, d), y_s.reshape(n_req, 1, d),
            k_p.reshape(batch, seq, N_HEADS, HEAD_DIM), v_p.reshape(batch, seq, N_HEADS, HEAD_DIM),
            lf_p.reshape(batch, seq, N_HEADS), conv_state_p,
            k_s.reshape(n_req, 1, N_HEADS, HEAD_DIM), v_s.reshape(n_req, 1, N_HEADS, HEAD_DIM),
            lf_s.reshape(n_req, 1, N_HEADS), conv_state_s)


def kernel(x_prompt, x_sample, cache_k, cache_v, cache_logf, state_conv, page_table, w_in, b_forget, conv_w, conv_b, conv_norm_g, conv_norm_b, w_attn_proj, w_conv_proj, w_out, ln1_g, ln1_b, w_router, b_router, w_gate_up, b_gate_up, w_down, b_down, ln2_g, ln2_b):
    assert x_prompt.shape[-1] == D_MODEL and w_in.shape[0] == DEPTH
    out = _layer(x_prompt, x_sample, cache_k[0], cache_v[0], cache_logf[0], state_conv[0], page_table,
                 w_in[0], b_forget[0], conv_w[0], conv_b[0], conv_norm_g[0], conv_norm_b[0],
                 w_attn_proj[0], w_conv_proj[0], w_out[0], ln1_g[0], ln1_b[0],
                 w_router[0], b_router[0], w_gate_up[0], b_gate_up[0], w_down[0], b_down[0],
                 ln2_g[0], ln2_b[0])
    y_p, y_s = out[0], out[1]
    return (y_p, y_s) + tuple(o[None] for o in out[2:])
```

```python
import functools
import math

import jax
import jax.numpy as jnp
from jax import lax
from jax.experimental import pallas as pl
from jax.experimental.pallas import tpu as pltpu

F32 = jnp.float32
BF16 = jnp.bfloat16

D_MODEL = 1024
N_HEADS = 16
HEAD_DIM = 64
CONV_WIDTH = 31
N_EXPERTS = 32
TOP_K = 4
D_FF = 1024
PAGE_SIZE = 128
SWIGLU_ALPHA = 1.702
SWIGLU_LIMIT = 7.0
LN_EPS = 1e-5
DEPTH = 1
DEEPNORM_ALPHA = (2 * DEPTH) ** 0.25
LOG2E = math.log2(math.e)

LANES = 128
HEADS_PER_LANE_TILE = LANES // HEAD_DIM
VMEM_LIMIT = 56 * 2 ** 20

TOKEN_TILE = 256
ATTN_TILE = 512
ROW_TILE = 256
PAGES_PER_STEP = 4
PAGE_CHUNK = 16
N_BIAS = 3
KEY_EXT = N_HEADS * LANES


def _params(*sem):
    return pltpu.CompilerParams(dimension_semantics=sem, vmem_limit_bytes=VMEM_LIMIT)


def _split3(x):
    hi = x.astype(BF16)
    r = x - hi.astype(F32)
    mid = r.astype(BF16)
    lo = (r - mid.astype(F32)).astype(BF16)
    return hi, mid, lo


def _dot(a, b):
    return jnp.dot(a, b, preferred_element_type=F32)


def _dot_exact_lhs(lhs_bf16, x):
    hi, mid, lo = _split3(x)
    return _dot(lhs_bf16, hi) + _dot(lhs_bf16, mid) + _dot(lhs_bf16, lo)


def _log_sigmoid(x):
    return jnp.minimum(x, 0.0) - jnp.log1p(jnp.exp(-jnp.abs(x)))


def _sigmoid(x):
    return 1.0 / (1.0 + jnp.exp(-x))


def _layer_norm(x, g, b):
    mu = jnp.mean(x, axis=-1, keepdims=True)
    xc = x - mu
    var = jnp.mean(xc * xc, axis=-1, keepdims=True)
    return xc * lax.rsqrt(var + LN_EPS) * g + b


def _tree(xs, op):
    xs = list(xs)
    while len(xs) > 1:
        xs = [op(xs[i], xs[i + 1]) if i + 1 < len(xs) else xs[i] for i in range(0, len(xs), 2)]
    return xs[0]


def _inproj_kernel(*refs, emit_keys, tiles_per_seq):
    if emit_keys:
        (x_ref, w_ref, wf_ref, wft_ref, bf_ref, bft_ref, place_ref,
         q_ref, k_ref, v_ref, lf_ref, lft_ref, u_ref, ga_ref, gc_ref, ke_ref, vb_ref, carry_ref) = refs
    else:
        (x_ref, w_ref, wf_ref, wft_ref, bf_ref, bft_ref,
         q_ref, k_ref, v_ref, lf_ref, lft_ref, u_ref, ga_ref, gc_ref) = refs
    tm = x_ref.shape[0]
    x = x_ref[...].astype(BF16)
    q = _dot(x, w_ref[0])
    q_ref[...] = (q * (LOG2E * HEAD_DIM ** -0.5)).astype(BF16)
    k = _dot(x, w_ref[1])
    k_ref[...] = k
    v = _dot(x, w_ref[2])
    v_ref[...] = v
    u_ref[...] = _dot(x, w_ref[3]) * _sigmoid(_dot(x, w_ref[4]))
    ga_ref[...] = _sigmoid(_dot(x, w_ref[5]))
    gc_ref[...] = _sigmoid(_dot(x, w_ref[6]))
    lf = _log_sigmoid(_dot(x, wf_ref[...]) + bf_ref[...])
    lf_ref[...] = lf
    ft = lax.dot_general(wft_ref[...], x, (((1,), (1,)), ((), ())),
                         preferred_element_type=F32)
    lft_ref[...] = _log_sigmoid(ft + bft_ref[...])
    if not emit_keys:
        return

    @pl.when(pl.program_id(0) % tiles_per_seq == 0)
    def _():
        carry_ref[...] = jnp.zeros(carry_ref.shape, F32)

    rr = lax.broadcasted_iota(jnp.int32, (tm, tm), 0)
    cc = lax.broadcasted_iota(jnp.int32, (tm, tm), 1)
    lower = jnp.where(cc <= rr, 1.0, 0.0).astype(BF16)
    csum = _dot_exact_lhs(lower, lf * LOG2E) + carry_ref[...]
    carry_ref[...] = csum[tm - 1:tm, :]
    hi, mid, lo = _split3(-csum)
    bias = _dot(hi, place_ref[0]) + _dot(mid, place_ref[1]) + _dot(lo, place_ref[2])
    k_dup = jnp.concatenate(
        [k[:, (t // 2) * LANES:(t // 2 + 1) * LANES] for t in range(N_HEADS)], axis=-1)
    lane = lax.broadcasted_iota(jnp.int32, (1, KEY_EXT), 1)
    own = ((lane // HEAD_DIM) % 2) == ((lane // LANES) % 2)
    ke_ref[...] = jnp.where(own, k_dup, bias).astype(BF16)
    vb_ref[...] = v.astype(BF16)


def _bias_placement():
    h = jnp.arange(N_HEADS)[None, :, None]
    p = jnp.arange(N_BIAS)[:, None, None]
    lane = jnp.arange(KEY_EXT)[None, None, :]
    spare = jnp.where(h % 2 == 0, HEAD_DIM, 0)
    return (lane == h * LANES + spare + p).astype(BF16)


def _in_projection(x, w_main, w_f, w_ft, b_f, b_ft, tm, seq=None):
    emit_keys = seq is not None
    n = x.shape[0]
    row = lambda i: (i, 0)
    const = lambda a: pl.BlockSpec(a.shape, lambda i: (0,) * a.ndim)
    big = lambda dt: jax.ShapeDtypeStruct((n, D_MODEL), dt)
    tile = pl.BlockSpec((tm, D_MODEL), row)
    out_shape = [big(BF16), big(F32), big(F32),
                 jax.ShapeDtypeStruct((n, N_HEADS), F32), jax.ShapeDtypeStruct((N_HEADS, n), F32),
                 big(F32), big(F32), big(F32)]
    out_specs = [tile, tile, tile, pl.BlockSpec((tm, N_HEADS), row),
                 pl.BlockSpec((N_HEADS, tm), lambda i: (0, i)), tile, tile, tile]
    args = [x, w_main, w_f, w_ft, b_f, b_ft]
    in_specs = [tile, pl.BlockSpec(w_main.shape, lambda i: (0, 0, 0), pipeline_mode=pl.Buffered(1)),
                const(w_f), const(w_ft), const(b_f), const(b_ft)]
    scratch = []
    if emit_keys:
        place = _bias_placement()
        args.append(place)
        in_specs.append(const(place))
        out_shape += [jax.ShapeDtypeStruct((n, KEY_EXT), BF16), big(BF16)]
        out_specs += [pl.BlockSpec((tm, KEY_EXT), row), tile]
        scratch = [pltpu.VMEM((1, N_HEADS), F32)]
    body = functools.partial(_inproj_kernel, emit_keys=emit_keys,
                             tiles_per_seq=(seq // tm if emit_keys else 1))
    return pl.pallas_call(body, grid=(n // tm,), in_specs=in_specs, out_specs=out_specs,
                          out_shape=out_shape, scratch_shapes=scratch,
                          compiler_params=_params("arbitrary"), name="in_projection")(*args)


def _flash_kernel(q_ref, ke_ref, v_ref, o_ref):
    t = ATTN_TILE
    i = pl.program_id(2)
    lane = lax.broadcasted_iota(jnp.int32, (1, LANES), 1)
    first = lane < HEAD_DIM
    ones_at = lambda cond: jnp.where(cond, 1.0, 0.0).astype(BF16)
    q2 = q_ref[...]
    q_ext = (jnp.where(first, q2, ones_at(lane < HEAD_DIM + N_BIAS)),
             jnp.where(first, ones_at(lane < N_BIAS), q2))
    den_lane = (HEAD_DIM, 0)
    v_fill = (ones_at(lane == den_lane[0]), ones_at(lane == den_lane[1]))
    nt = (((1,), (1,)), ((), ()))

    def step(j, carry, masked):
        ks = pl.multiple_of(j * t, t)
        v2 = v_ref[pl.ds(ks, t), :]
        v_ext = (jnp.where(first, v2, v_fill[0]), jnp.where(first, v_fill[1], v2))
        out = []
        for h in range(HEADS_PER_LANE_TILE):
            m, acc = carry[2 * h], carry[2 * h + 1]
            k_h = ke_ref[pl.ds(ks, t), h * LANES:(h + 1) * LANES]
            s = lax.dot_general(q_ext[h], k_h, nt, preferred_element_type=F32)
            if masked:
                rr = lax.broadcasted_iota(jnp.int32, (t, t), 0)
                cc = lax.broadcasted_iota(jnp.int32, (t, t), 1)
                s = jnp.where(cc <= rr, s, -jnp.inf)
            m_new = jnp.maximum(m, jnp.max(s, axis=-1, keepdims=True))
            alpha = jnp.exp2(m - m_new)
            p = jnp.exp2(s - m_new).astype(BF16)
            out += [m_new, alpha * acc + _dot(p, v_ext[h])]
        return tuple(out)

    neg = jnp.full((t, 1), -jnp.inf, F32)
    za = jnp.zeros((t, LANES), F32)
    carry = lax.fori_loop(0, i, lambda j, c: step(j, c, False), (neg, za, neg, za))
    _, acc0, _, acc1 = step(i, carry, True)
    inv0 = 1.0 / acc0[:, den_lane[0]:den_lane[0] + 1]
    inv1 = 1.0 / acc1[:, den_lane[1]:den_lane[1] + 1]
    o_ref[...] = jnp.where(first, acc0 * inv0, acc1 * inv1).astype(o_ref.dtype)


def _prompt_attention(qb, ke, vb, batch, seq):
    t = ATTN_TILE
    nq = seq // t
    pairs = N_HEADS // HEADS_PER_LANE_TILE
    q_spec = pl.BlockSpec((t, LANES), lambda b, hp, i: (b * nq + i, hp))
    ke_spec = pl.BlockSpec((seq, HEADS_PER_LANE_TILE * LANES), lambda b, hp, i: (b, hp))
    v_spec = pl.BlockSpec((seq, LANES), lambda b, hp, i: (b, hp))
    return pl.pallas_call(_flash_kernel, grid=(batch, pairs, nq),
                          in_specs=[q_spec, ke_spec, v_spec], out_specs=q_spec,
                          out_shape=jax.ShapeDtypeStruct(qb.shape, BF16),
                          compiler_params=_params("parallel", "parallel", "arbitrary"),
                          name="prompt_attention")(qb, ke, vb)


def _paged_kernel(pt_ref, *refs):
    pp = PAGES_PER_STEP
    ch = PAGE_CHUNK
    k_refs, v_refs, lf_refs = refs[:pp], refs[pp:2 * pp], refs[2 * pp:3 * pp]
    q_ref, kn_ref, vn_ref, lfnt_ref, o_ref, m_ref, l_ref, c_ref, acc_ref, lg_ref, hot_ref = refs[3 * pp:]
    r = pl.program_id(0)
    g = pl.program_id(1)
    half = PAGE_SIZE // 2
    wide2 = (N_HEADS, PAGE_SIZE)

    @pl.when(g == 0)
    def _():
        m_ref[...] = jnp.full(m_ref.shape, -jnp.inf, F32)
        l_ref[...] = jnp.zeros(l_ref.shape, F32)
        c_ref[...] = jnp.zeros(c_ref.shape, F32)
        acc_ref[...] = jnp.zeros(acc_ref.shape, F32)
        lane3 = lax.broadcasted_iota(jnp.int32, hot_ref.shape, 2)
        t3 = lax.broadcasted_iota(jnp.int32, hot_ref.shape, 0)
        hot_ref[...] = jnp.where(lane3 == half + t3 % half, -1.0, 0.0)

    q = q_ref[0].astype(F32)
    rr = lax.broadcasted_iota(jnp.int32, (PAGE_SIZE, PAGE_SIZE), 0)
    cc = lax.broadcasted_iota(jnp.int32, (PAGE_SIZE, PAGE_SIZE), 1)
    upper = jnp.where(rr <= cc, 1.0, 0.0).astype(BF16)
    tn = (((0,), (0,)), ((), ()))

    def one_page(p_i, state):
        m_old, l_old, c_old, acc_old = state
        k_ref, v_ref = k_refs[p_i], v_refs[p_i]
        hi, mid, lo = _split3(lf_refs[p_i][0] * LOG2E)
        incl_t = (lax.dot_general(hi, upper, tn, preferred_element_type=F32)
                  + lax.dot_general(mid, upper, tn, preferred_element_type=F32)
                  + lax.dot_general(lo, upper, tn, preferred_element_type=F32)) + c_old
        c_new = jnp.broadcast_to(incl_t[:, PAGE_SIZE - 1:PAGE_SIZE], wide2)
        incl_sw = jnp.concatenate([incl_t[:, half:], incl_t[:, :half]], axis=-1)
        maxes = []
        for c0 in range(0, PAGE_SIZE, ch):
            src = incl_sw if c0 < half else incl_t
            kq = k_ref[0, c0:c0 + ch] * q[None]
            wide = jnp.concatenate([kq, jnp.zeros_like(kq)], axis=-1) + src[None] * hot_ref[c0:c0 + ch]
            logit = jnp.broadcast_to(jnp.sum(wide, axis=-1, keepdims=True), (ch, N_HEADS, PAGE_SIZE))
            lg_ref[p_i, c0:c0 + ch] = logit
            maxes.append(_tree([logit[i] for i in range(ch)], jnp.maximum))
        m_new = jnp.maximum(m_old, _tree(maxes, jnp.maximum))
        alpha = jnp.exp2(m_old - m_new)
        ls, accs = [], []
        for c0 in range(0, PAGE_SIZE, ch):
            p3 = jnp.exp2(lg_ref[p_i, c0:c0 + ch] - m_new[None])
            pv = p3[:, :, :HEAD_DIM] * v_ref[0, c0:c0 + ch]
            ls.append(_tree([p3[i] for i in range(ch)], jnp.add))
            accs.append(_tree([pv[i] for i in range(ch)], jnp.add))
        l_new = alpha * l_old + _tree(ls, jnp.add)
        acc_new = alpha[:, :HEAD_DIM] * acc_old + _tree(accs, jnp.add)
        return m_new, l_new, c_new, acc_new

    state = (m_ref[...], l_ref[...], c_ref[...], acc_ref[...])
    for p_i in range(pp):
        state = one_page(p_i, state)
    m_ref[...], l_ref[...], c_ref[...], acc_ref[...] = state

    @pl.when(g == pl.num_programs(1) - 1)
    def _():
        s_new = jnp.broadcast_to(jnp.sum(kn_ref[0] * q, axis=-1, keepdims=True), wide2)
        lane = lax.broadcasted_iota(jnp.int32, lfnt_ref.shape, 1)
        lf_new = jnp.broadcast_to(
            jnp.sum(jnp.where(lane == r, lfnt_ref[...], 0.0), axis=-1, keepdims=True), wide2)
        logit = s_new - (c_ref[...] + lf_new * LOG2E)
        m_old = m_ref[...]
        m_new = jnp.maximum(m_old, logit)
        alpha = jnp.exp2(m_old - m_new)
        p_new = jnp.exp2(logit - m_new)
        l_fin = alpha * l_ref[...] + p_new
        acc = alpha[:, :HEAD_DIM] * acc_ref[...] + p_new[:, :HEAD_DIM] * vn_ref[0]
        o_ref[0] = (acc / l_fin[:, :HEAD_DIM]).astype(o_ref.dtype)


def _sample_attention(q_s, k_s, v_s, lft_s, cache_k, cache_v, cache_logf, page_table):
    n_req, n_pages = page_table.shape
    pp = PAGES_PER_STEP
    steps = n_pages // pp

    def page_map(p_i, nd):
        return lambda r, g, pt: (pt[r * n_pages + g * pp + p_i],) + (0,) * nd

    k_specs = [pl.BlockSpec((1, PAGE_SIZE, N_HEADS, HEAD_DIM), page_map(p, 3)) for p in range(pp)]
    lf_specs = [pl.BlockSpec((1, PAGE_SIZE, N_HEADS), page_map(p, 2)) for p in range(pp)]
    head = pl.BlockSpec((1, N_HEADS, HEAD_DIM), lambda r, g, pt: (r, 0, 0))
    wide = lambda: pltpu.VMEM((N_HEADS, PAGE_SIZE), F32)
    grid_spec = pltpu.PrefetchScalarGridSpec(
        num_scalar_prefetch=1, grid=(n_req, steps),
        in_specs=k_specs + k_specs + lf_specs
        + [head, head, head, pl.BlockSpec(lft_s.shape, lambda r, g, pt: (0, 0))],
        out_specs=head,
        scratch_shapes=[wide(), wide(), wide(), pltpu.VMEM((N_HEADS, HEAD_DIM), F32),
                        pltpu.VMEM((pp, PAGE_SIZE, N_HEADS, PAGE_SIZE), F32),
                        pltpu.VMEM((PAGE_SIZE, N_HEADS, PAGE_SIZE), F32)])
    args = [cache_k] * pp + [cache_v] * pp + [cache_logf] * pp + [q_s, k_s, v_s, lft_s]
    return pl.pallas_call(_paged_kernel, grid_spec=grid_spec,
                          out_shape=jax.ShapeDtypeStruct((n_req, N_HEADS, HEAD_DIM), BF16),
                          compiler_params=_params("arbitrary", "arbitrary"),
                          name="sample_attention")(page_table.reshape(-1), *args)


def _conv_prompt_kernel(u_ref, w_ref, b_ref, g_ref, bn_ref, o_ref, buf_ref, y_ref):
    ts = u_ref.shape[0]
    halo = 32
    j = pl.program_id(1)

    @pl.when(j == 0)
    def _():
        buf_ref[0:halo, :] = jnp.zeros((halo, D_MODEL), F32)

    buf_ref[halo:halo + ts, :] = u_ref[...]
    for c in range(D_MODEL // LANES):
        sl = slice(c * LANES, (c + 1) * LANES)
        acc = jnp.zeros((ts, LANES), F32)
        for tap in range(CONV_WIDTH):
            off = halo - (CONV_WIDTH - 1) + tap
            acc = acc + buf_ref[off:off + ts, sl] * w_ref[tap:tap + 1, sl]
        y_ref[:, sl] = acc + b_ref[:, sl]
    buf_ref[0:halo, :] = buf_ref[ts:ts + halo, :]
    y = _layer_norm(y_ref[...], g_ref[...], bn_ref[...])
    o_ref[...] = (y * _sigmoid(y)).astype(o_ref.dtype)


def _conv_prompt(u, conv_w, conv_b, g, bn, batch, seq):
    ts = TOKEN_TILE
    ns = seq // ts
    tile = pl.BlockSpec((ts, D_MODEL), lambda b, j: (b * ns + j, 0))
    const = lambda a: pl.BlockSpec(a.shape, lambda b, j: (0, 0))
    return pl.pallas_call(_conv_prompt_kernel, grid=(batch, ns),
                          in_specs=[tile, const(conv_w), const(conv_b), const(g), const(bn)],
                          out_specs=tile, out_shape=jax.ShapeDtypeStruct(u.shape, BF16),
                          scratch_shapes=[pltpu.VMEM((32 + ts, D_MODEL), F32),
                                          pltpu.VMEM((ts, D_MODEL), F32)],
                          compiler_params=_params("parallel", "arbitrary"),
                          name="conv_prompt")(u, conv_w, conv_b, g, bn)


def _conv_sample_kernel(state_ref, u_ref, w_ref, b_ref, g_ref, bn_ref, o_ref):
    hist = CONV_WIDTH - 1
    y = jnp.sum(state_ref[...] * w_ref[0:hist, :][None], axis=1)
    y = y + u_ref[...] * w_ref[hist:hist + 1, :] + b_ref[...]
    y = _layer_norm(y, g_ref[...], bn_ref[...])
    o_ref[...] = (y * _sigmoid(y)).astype(o_ref.dtype)


def _conv_sample(state, u, conv_w, conv_b, g, bn):
    full = lambda a: pl.BlockSpec(a.shape, lambda i: (0,) * a.ndim)
    args = (state, u, conv_w, conv_b, g, bn)
    return pl.pallas_call(_conv_sample_kernel, grid=(1,), in_specs=[full(a) for a in args],
                          out_specs=full(u), out_shape=jax.ShapeDtypeStruct(u.shape, BF16),
                          compiler_params=_params("arbitrary"), name="conv_sample")(*args)


def _merge_router_kernel(att_ref, conv_ref, ga_ref, gc_ref, x_ref, wa_ref, wc_ref, wo_ref,
                         g1_ref, b1_ref, wr_ref, br_ref, cnt_in_ref,
                         h_ref, idx_ref, wts_ref, rank_ref, cnt_ref):
    tm = x_ref.shape[0]

    @pl.when(pl.program_id(0) == 0)
    def _():
        cnt_ref[...] = cnt_in_ref[...]

    a = _dot(att_ref[...], wa_ref[...])
    c = _dot(conv_ref[...], wc_ref[...])
    mixed = (ga_ref[...] * a + gc_ref[...] * c).astype(BF16)
    res = DEEPNORM_ALPHA * x_ref[...] + _dot(mixed, wo_ref[...])
    h = _layer_norm(res, g1_ref[...], b1_ref[...])
    h_ref[...] = h

    hh, hm, hl = _split3(h)
    w_hi, w_mid, w_lo = wr_ref[0], wr_ref[1], wr_ref[2]
    logits = (_dot(hh, w_hi) + _dot(hh, w_mid) + _dot(hm, w_hi)
              + _dot(hh, w_lo) + _dot(hm, w_mid) + _dot(hl, w_hi)) + br_ref[...]

    eid = lax.broadcasted_iota(jnp.int32, (tm, N_EXPERTS), 1).astype(F32)
    k_lane = lax.broadcasted_iota(jnp.int32, (tm, TOP_K), 1)
    remaining = logits
    chosen = jnp.zeros((tm, N_EXPERTS), F32)
    vals, picks = [], []
    for _ in range(TOP_K):
        mx = jnp.max(remaining, axis=-1, keepdims=True)
        pick = jnp.min(jnp.where(remaining == mx, eid, N_EXPERTS), axis=-1, keepdims=True)
        hit = eid == pick
        chosen = jnp.where(hit, 1.0, chosen)
        remaining = jnp.where(hit, -jnp.inf, remaining)
        vals.append(mx)
        picks.append(pick)
    exps = [jnp.exp(v - vals[0]) for v in vals]
    denom = exps[0] + exps[1] + exps[2] + exps[3]

    rr = lax.broadcasted_iota(jnp.int32, (tm, tm), 0)
    cc = lax.broadcasted_iota(jnp.int32, (tm, tm), 1)
    strict_lower = jnp.where(cc < rr, 1.0, 0.0).astype(BF16)
    rank_dense = _dot(strict_lower, chosen.astype(BF16)) + cnt_ref[...]

    idx_out = jnp.zeros((tm, TOP_K), F32)
    wts_out = jnp.zeros((tm, TOP_K), F32)
    rank_out = jnp.zeros((tm, TOP_K), F32)
    for k in range(TOP_K):
        rk = jnp.sum(jnp.where(eid == picks[k], rank_dense, 0.0), axis=-1, keepdims=True)
        idx_out = jnp.where(k_lane == k, picks[k], idx_out)
        wts_out = jnp.where(k_lane == k, exps[k] / denom, wts_out)
        rank_out = jnp.where(k_lane == k, rk, rank_out)
    idx_ref[...] = idx_out.astype(jnp.int32)
    wts_ref[...] = wts_out
    rank_ref[...] = rank_out.astype(jnp.int32)
    cnt_ref[...] = cnt_ref[...] + jnp.sum(chosen, axis=0, keepdims=True)


def _merge_router(att, conv, ga, gc, x, wa, wc, wo, g1, b1, wr3, br, cnt_in, tm):
    n = x.shape[0]
    row = lambda i: (i, 0)
    tile = pl.BlockSpec((tm, D_MODEL), row)
    small = pl.BlockSpec((tm, TOP_K), row)
    const = lambda a: pl.BlockSpec(a.shape, lambda i: (0,) * a.ndim)
    out_shape = (jax.ShapeDtypeStruct((n, D_MODEL), F32),
                 jax.ShapeDtypeStruct((n, TOP_K), jnp.int32),
                 jax.ShapeDtypeStruct((n, TOP_K), F32),
                 jax.ShapeDtypeStruct((n, TOP_K), jnp.int32),
                 jax.ShapeDtypeStruct((1, N_EXPERTS), F32))
    return pl.pallas_call(
        _merge_router_kernel, grid=(n // tm,),
        in_specs=[tile, tile, tile, tile, tile, const(wa), const(wc), const(wo),
                  const(g1), const(b1), const(wr3), const(br), const(cnt_in)],
        out_specs=(tile, small, small, small, const(cnt_in)), out_shape=out_shape,
        compiler_params=_params("arbitrary"), name="merge_router",
    )(att, conv, ga, gc, x, wa, wc, wo, g1, b1, wr3, br, cnt_in)


def _row_copy(src_ref, src_row, dst_ref, dst_row, sem):
    return pltpu.make_async_copy(src_ref.at[pl.ds(src_row, 1), :], dst_ref.at[pl.ds(dst_row, 1), :], sem)


def _dispatch_kernel(pos_ref, h_ref, xs_in_ref, xs_ref, sem):
    del xs_in_ref
    tm = h_ref.shape[0]

    def issue(t, carry):
        for k in range(TOP_K):
            _row_copy(h_ref, t, xs_ref, pos_ref[0, 0, t * TOP_K + k], sem).start()
        return carry

    lax.fori_loop(0, tm, issue, 0)
    for _ in range(TOP_K):
        pltpu.make_async_copy(h_ref, xs_ref.at[pl.ds(0, tm), :], sem).wait()


def _dispatch(pos, h, xs, tm):
    n = h.shape[0]
    pos3 = pos.reshape(n // tm, 1, tm * TOP_K)
    return pl.pallas_call(
        _dispatch_kernel, grid=(n // tm,),
        in_specs=[pl.BlockSpec((1, 1, tm * TOP_K), lambda i: (i, 0, 0), memory_space=pltpu.SMEM),
                  pl.BlockSpec((tm, D_MODEL), lambda i: (i, 0)),
                  pl.BlockSpec(memory_space=pl.ANY)],
        out_specs=pl.BlockSpec(memory_space=pl.ANY),
        out_shape=jax.ShapeDtypeStruct(xs.shape, xs.dtype),
        scratch_shapes=[pltpu.SemaphoreType.DMA(())],
        input_output_aliases={2: 0},
        compiler_params=_params("arbitrary"), name="moe_dispatch")(pos3, h, xs)


def _experts_kernel(te_ref, tv_ref, xs_ref, wgu_ref, bgu_ref, wd_ref, bd_ref, o_ref, wgu_bf, wd_bf):
    n = pl.program_id(0)
    prev = te_ref[jnp.maximum(n - 1, 0)]

    @pl.when((n == 0) | (te_ref[n] != prev))
    def _():
        wgu_bf[...] = wgu_ref[0].astype(BF16)
        wd_bf[...] = wd_ref[0].astype(BF16)

    @pl.when(tv_ref[n] == 1)
    def _():
        gu = _dot(xs_ref[...].astype(BF16), wgu_bf[...]) + bgu_ref[0]
        gate = jnp.minimum(gu[:, :D_FF], SWIGLU_LIMIT)
        up = jnp.clip(gu[:, D_FF:], -SWIGLU_LIMIT, SWIGLU_LIMIT)
        act = (up + 1.0) * gate * _sigmoid(SWIGLU_ALPHA * gate)
        o_ref[...] = _dot(act.astype(BF16), wd_bf[...]) + bd_ref[0]

    @pl.when(tv_ref[n] == 0)
    def _():
        o_ref[...] = jnp.zeros(o_ref.shape, F32)


def _experts(tile_expert, tile_valid, xs, w_gate_up, b_gate_up, w_down, b_down):
    rows = xs.shape[0]
    tm = ROW_TILE
    e3 = lambda n, te, tv: (te[n], 0, 0)
    row = lambda n, te, tv: (n, 0)
    grid_spec = pltpu.PrefetchScalarGridSpec(
        num_scalar_prefetch=2, grid=(rows // tm,),
        in_specs=[pl.BlockSpec((tm, D_MODEL), row),
                  pl.BlockSpec((1, D_MODEL, 2 * D_FF), e3),
                  pl.BlockSpec((1, 1, 2 * D_FF), e3),
                  pl.BlockSpec((1, D_FF, D_MODEL), e3),
                  pl.BlockSpec((1, 1, D_MODEL), e3)],
        out_specs=pl.BlockSpec((tm, D_MODEL), row),
        scratch_shapes=[pltpu.VMEM((D_MODEL, 2 * D_FF), BF16), pltpu.VMEM((D_FF, D_MODEL), BF16)])
    return pl.pallas_call(_experts_kernel, grid_spec=grid_spec,
                          out_shape=jax.ShapeDtypeStruct((rows, D_MODEL), F32),
                          compiler_params=_params("arbitrary"), name="moe_experts",
                          )(tile_expert, tile_valid, xs, w_gate_up,
                            b_gate_up.reshape(N_EXPERTS, 1, 2 * D_FF), w_down,
                            b_down.reshape(N_EXPERTS, 1, D_MODEL))


def _combine_kernel(pos_ref, wts_ref, h_ref, ys_ref, g_ref, b_ref, o_ref, buf_ref, sem):
    tm = h_ref.shape[0]

    def issue(t, carry):
        for k in range(TOP_K):
            _row_copy(ys_ref, pos_ref[0, 0, t * TOP_K + k], buf_ref.at[k], t, sem).start()
        return carry

    lax.fori_loop(0, tm, issue, 0)
    for k in range(TOP_K):
        pltpu.make_async_copy(ys_ref.at[pl.ds(0, tm), :], buf_ref.at[k], sem).wait()
    wts = wts_ref[...]
    moe = wts[:, 0:1] * buf_ref[0]
    for k in range(1, TOP_K):
        moe = moe + wts[:, k:k + 1] * buf_ref[k]
    o_ref[...] = _layer_norm(DEEPNORM_ALPHA * h_ref[...] + moe, g_ref[...], b_ref[...])


def _combine(pos, wts, h, ys, g2, b2, tm):
    n = h.shape[0]
    pos3 = pos.reshape(n // tm, 1, tm * TOP_K)
    row = lambda i: (i, 0)
    const = lambda a: pl.BlockSpec(a.shape, lambda i: (0,) * a.ndim)
    return pl.pallas_call(
        _combine_kernel, grid=(n // tm,),
        in_specs=[pl.BlockSpec((1, 1, tm * TOP_K), lambda i: (i, 0, 0), memory_space=pltpu.SMEM),
                  pl.BlockSpec((tm, TOP_K), row),
                  pl.BlockSpec((tm, D_MODEL), row),
                  pl.BlockSpec(memory_space=pl.ANY), const(g2), const(b2)],
        out_specs=pl.BlockSpec((tm, D_MODEL), row),
        out_shape=jax.ShapeDtypeStruct((n, D_MODEL), F32),
        scratch_shapes=[pltpu.VMEM((TOP_K, tm, D_MODEL), F32), pltpu.SemaphoreType.DMA(())],
        compiler_params=_params("arbitrary"), name="moe_combine")(pos3, wts, h, ys, g2, b2)


def _split_in_proj(w_in, b_forget):
    a = N_HEADS * HEAD_DIM
    cuts = [0, a, 2 * a, 3 * a]
    f0 = 3 * a
    rest = f0 + N_HEADS
    starts = cuts[:3] + [rest + i * D_MODEL for i in range(4)]
    w_main = jnp.stack([w_in[:, s:s + D_MODEL] for s in starts]).astype(BF16)
    w_f = w_in[:, f0:rest].astype(BF16)
    return w_main, w_f, w_f.T, b_forget.reshape(1, N_HEADS), b_forget.reshape(N_HEADS, 1)


def _routing_tables(counts, n_tiles):
    cnt = counts.reshape(N_EXPERTS).astype(jnp.int32)
    tiles = (cnt + ROW_TILE - 1) // ROW_TILE
    tile_end = jnp.cumsum(tiles)
    start_row = (tile_end - tiles) * ROW_TILE
    n = jnp.arange(n_tiles, dtype=jnp.int32)
    valid = n < tile_end[-1]
    owner = jnp.sum((n[:, None] >= tile_end[None, :]).astype(jnp.int32), axis=1)
    last_owner = jnp.sum((tile_end[-1] - 1 >= tile_end).astype(jnp.int32))
    tile_expert = jnp.where(valid, owner, last_owner).astype(jnp.int32)
    return start_row, tile_expert, valid.astype(jnp.int32)


def _layer(xp, xs, cache_k, cache_v, cache_logf, state_conv, page_table,
           w_in, b_forget, conv_w, conv_b, conv_norm_g, conv_norm_b,
           w_attn_proj, w_conv_proj, w_out, ln1_g, ln1_b,
           w_router, b_router, w_gate_up, b_gate_up, w_down, b_down, ln2_g, ln2_b):
    batch, seq, d = xp.shape
    n_req = xs.shape[0]
    n_p = batch * seq
    row = lambda a: a.reshape(1, -1)
    heads = lambda a: a.reshape(a.shape[0], N_HEADS, HEAD_DIM)

    w_main, w_f, w_ft, b_f, b_ft = _split_in_proj(w_in, b_forget)
    wa, wc, wo = (w.astype(BF16) for w in (w_attn_proj, w_conv_proj, w_out))
    wr3 = jnp.stack(_split3(w_router))
    cw, cb, cg, cbn = conv_w, row(conv_b), row(conv_norm_g), row(conv_norm_b)
    g1, b1, g2, b2, br = row(ln1_g), row(ln1_b), row(ln2_g), row(ln2_b), row(b_router)

    xp2 = xp.reshape(n_p, d)
    qb, k_p, v_p, lf_p, _, u_p, ga_p, gc_p, ke, vb = _in_projection(
        xp2, w_main, w_f, w_ft, b_f, b_ft, TOKEN_TILE, seq)
    att_p = _prompt_attention(qb, ke, vb, batch, seq)
    conv_p = _conv_prompt(u_p, cw, cb, cg, cbn, batch, seq)

    xs2 = xs.reshape(n_req, d)
    q_s, k_s, v_s, lf_s, lft_s, u_s, ga_s, gc_s = _in_projection(
        xs2, w_main, w_f, w_ft, b_f, b_ft, n_req)
    att_s = _sample_attention(heads(q_s), heads(k_s), heads(v_s), lft_s,
                              cache_k, cache_v, cache_logf, page_table).reshape(n_req, d)
    conv_s = _conv_sample(state_conv, u_s, cw, cb, cg, cbn)

    zero_cnt = jnp.zeros((1, N_EXPERTS), F32)
    h_p, idx_p, wts_p, rank_p, cnt_p = _merge_router(
        att_p, conv_p, ga_p, gc_p, xp2, wa, wc, wo, g1, b1, wr3, br, zero_cnt, TOKEN_TILE)
    h_s, idx_s, wts_s, rank_s, cnt = _merge_router(
        att_s, conv_s, ga_s, gc_s, xs2, wa, wc, wo, g1, b1, wr3, br, cnt_p, n_req)

    n_tok = n_p + n_req
    n_tiles = (n_tok * TOP_K + N_EXPERTS * (ROW_TILE - 1) + ROW_TILE - 1) // ROW_TILE
    start_row, tile_expert, tile_valid = _routing_tables(cnt, n_tiles)
    pos_p = start_row[idx_p] + rank_p
    pos_s = start_row[idx_s] + rank_s
    sorted_rows = jnp.zeros((n_tiles * ROW_TILE, d), F32)
    sorted_rows = _dispatch(pos_p, h_p, sorted_rows, TOKEN_TILE)
    sorted_rows = _dispatch(pos_s, h_s, sorted_rows, n_req)
    expert_out = _experts(tile_expert, tile_valid, sorted_rows, w_gate_up, b_gate_up, w_down, b_down)
    y_p = _combine(pos_p, wts_p, h_p, expert_out, g2, b2, TOKEN_TILE)
    y_s = _combine(pos_s, wts_s, h_s, expert_out, g2, b2, n_req)

    hist = CONV_WIDTH - 1
    conv_state_p = u_p.reshape(batch, seq, d)[:, seq - hist:, :]
    conv_state_s = jnp.concatenate([state_conv[:, 1:, :], u_s[:, None, :]], axis=1)
    return (y_p.reshape(batch, seq, d), y_s.reshape(n_req, 1, d),
            k_p.reshape(batch, seq, N_HEADS, HEAD_DIM), v_p.reshape(batch, seq, N_HEADS, HEAD_DIM),
            lf_p.reshape(batch, seq, N_HEADS), conv_state_p,
            k_s.reshape(n_req, 1, N_HEADS, HEAD_DIM), v_s.reshape(n_req, 1, N_HEADS, HEAD_DIM),
            lf_s.reshape(n_req, 1, N_HEADS), conv_state_s)


def kernel(x_prompt, x_sample, cache_k, cache_v, cache_logf, state_conv, page_table, w_in, b_forget, conv_w, conv_b, conv_norm_g, conv_norm_b, w_attn_proj, w_conv_proj, w_out, ln1_g, ln1_b, w_router, b_router, w_gate_up, b_gate_up, w_down, b_down, ln2_g, ln2_b):
    assert x_prompt.shape[-1] == D_MODEL and w_in.shape[0] == DEPTH
    out = _layer(x_prompt, x_sample, cache_k[0], cache_v[0], cache_logf[0], state_conv[0], page_table,
                 w_in[0], b_forget[0], conv_w[0], conv_b[0], conv_norm_g[0], conv_norm_b[0],
                 w_attn_proj[0], w_conv_proj[0], w_out[0], ln1_g[0], ln1_b[0],
                 w_router[0], b_router[0], w_gate_up[0], b_gate_up[0], w_down[0], b_down[0],
                 ln2_g[0], ln2_b[0])
    y_p, y_s = out[0], out[1]
    return (y_p, y_s) + tuple(o[None] for o in out[2:])
```

```python
import functools
import math

import jax
import jax.numpy as jnp
from jax import lax
from jax.experimental import pallas as pl
from jax.experimental.pallas import tpu as pltpu

F32 = jnp.float32
BF16 = jnp.bfloat16

D_MODEL = 1024
N_HEADS = 16
HEAD_DIM = 64
CONV_WIDTH = 31
N_EXPERTS = 32
TOP_K = 4
D_FF = 1024
PAGE_SIZE = 128
SWIGLU_ALPHA = 1.702
SWIGLU_LIMIT = 7.0
LN_EPS = 1e-5
DEPTH = 1
DEEPNORM_ALPHA = (2 * DEPTH) ** 0.25
LOG2E = math.log2(math.e)

LANES = 128
SUBLANES = 8
HEADS_PER_LANE_TILE = LANES // HEAD_DIM
VMEM_LIMIT = 56 * 2 ** 20

TOKEN_TILE = 256
ATTN_TILE = 512
ROW_TILE = 256
PAGES_PER_STEP = 8
N_BIAS = 3
KEY_EXT = N_HEADS * LANES


def _params(*sem):
    return pltpu.CompilerParams(dimension_semantics=sem, vmem_limit_bytes=VMEM_LIMIT)


def _split3(x):
    hi = x.astype(BF16)
    r = x - hi.astype(F32)
    mid = r.astype(BF16)
    lo = (r - mid.astype(F32)).astype(BF16)
    return hi, mid, lo


def _dot(a, b):
    return jnp.dot(a, b, preferred_element_type=F32)


def _cumsum_lanes(x, upper, carry):
    hi, mid, lo = _split3(x)
    return _dot(hi, upper) + _dot(mid, upper) + _dot(lo, upper) + carry


def _upper_ones(n):
    rr = lax.broadcasted_iota(jnp.int32, (n, n), 0)
    cc = lax.broadcasted_iota(jnp.int32, (n, n), 1)
    return jnp.where(rr <= cc, 1.0, 0.0).astype(BF16)


def _log_sigmoid(x):
    return jnp.minimum(x, 0.0) - jnp.log1p(jnp.exp(-jnp.abs(x)))


def _sigmoid(x):
    return 1.0 / (1.0 + jnp.exp(-x))


def _layer_norm(x, g, b):
    mu = jnp.mean(x, axis=-1, keepdims=True)
    xc = x - mu
    var = jnp.mean(xc * xc, axis=-1, keepdims=True)
    return xc * lax.rsqrt(var + LN_EPS) * g + b


def _tree(xs, op):
    xs = list(xs)
    while len(xs) > 1:
        xs = [op(xs[i], xs[i + 1]) if i + 1 < len(xs) else xs[i] for i in range(0, len(xs), 2)]
    return xs[0]


def _extended_rows(rows, extra, head):
    pad = jnp.zeros((HEAD_DIM - SUBLANES, rows.shape[1]), F32)
    return [rows, extra, pad] if head % 2 == 0 else [extra, pad, rows]


def _inproj_kernel(*refs, emit_keys, tiles_per_seq):
    if emit_keys:
        (x_ref, w_ref, wf_ref, wft_ref, bf_ref, bft_ref, wt_ref,
         q_ref, u_ref, ga_ref, gc_ref, kt_ref, vt_ref, lft_ref, ke_ref, ve_ref, carry_ref) = refs
    else:
        (x_ref, w_ref, wf_ref, wft_ref, bf_ref, bft_ref,
         q_ref, u_ref, ga_ref, gc_ref, k_ref, v_ref, lf_ref, lft_ref) = refs
    tm = x_ref.shape[0]
    nt = (((1,), (1,)), ((), ()))
    x = x_ref[...].astype(BF16)
    q = _dot(x, w_ref[0])
    q_ref[...] = (q * (LOG2E * HEAD_DIM ** -0.5)).astype(BF16)
    u_ref[...] = _dot(x, w_ref[3]) * _sigmoid(_dot(x, w_ref[4]))
    ga_ref[...] = _sigmoid(_dot(x, w_ref[5]))
    gc_ref[...] = _sigmoid(_dot(x, w_ref[6]))
    ft = lax.dot_general(wft_ref[...], x, nt, preferred_element_type=F32)
    lft = _log_sigmoid(ft + bft_ref[...])
    if not emit_keys:
        k_ref[...] = _dot(x, w_ref[1])
        v_ref[...] = _dot(x, w_ref[2])
        lf_ref[...] = _log_sigmoid(_dot(x, wf_ref[...]) + bf_ref[...])
        lft_ref[...] = lft
        return

    kt = lax.dot_general(wt_ref[0], x, nt, preferred_element_type=F32)
    vt = lax.dot_general(wt_ref[1], x, nt, preferred_element_type=F32)
    kt_ref[0] = kt
    vt_ref[0] = vt
    lft_ref[0] = lft

    @pl.when(pl.program_id(0) % tiles_per_seq == 0)
    def _():
        carry_ref[...] = jnp.zeros(carry_ref.shape, F32)

    csum = _cumsum_lanes(lft * LOG2E, _upper_ones(tm), carry_ref[...])
    carry_ref[...] = csum[:, tm - 1:tm]
    pieces = [p.astype(F32) for p in _split3(-csum)]
    sub = lax.broadcasted_iota(jnp.int32, (SUBLANES, tm), 0)
    ones_rows = jnp.where(sub == 0, 1.0, 0.0)
    k_blocks, v_blocks = [], []
    for h in range(N_HEADS):
        bias_rows = jnp.zeros((SUBLANES, tm), F32)
        for i, piece in enumerate(pieces):
            bias_rows = jnp.where(sub == i, piece[h:h + 1, :], bias_rows)
        feat = slice(h * HEAD_DIM, (h + 1) * HEAD_DIM)
        k_blocks += _extended_rows(kt[feat], bias_rows, h)
        v_blocks += _extended_rows(vt[feat], ones_rows, h)
    ke_ref[0] = jnp.concatenate(k_blocks, axis=0).astype(BF16)
    ve_ref[0] = jnp.concatenate(v_blocks, axis=0).astype(BF16)


def _in_projection(x, w_main, w_f, w_ft, b_f, b_ft, tm, w_t=None, seq=None):
    emit_keys = seq is not None
    n = x.shape[0]
    row = lambda i: (i, 0)
    const = lambda a: pl.BlockSpec(a.shape, lambda i: (0,) * a.ndim)
    once = lambda a: pl.BlockSpec(a.shape, lambda i: (0,) * a.ndim, pipeline_mode=pl.Buffered(1))
    big = lambda dt: jax.ShapeDtypeStruct((n, D_MODEL), dt)
    tile = pl.BlockSpec((tm, D_MODEL), row)
    args = [x, w_main, w_f, w_ft, b_f, b_ft]
    in_specs = [tile, once(w_main), const(w_f), const(w_ft), const(b_f), const(b_ft)]
    out_shape = [big(BF16), big(F32), big(F32), big(F32)]
    out_specs = [tile, tile, tile, tile]
    scratch = []
    if emit_keys:
        tiles_per_seq = seq // tm
        batch = n // seq
        seq_major = lambda rows, dt: jax.ShapeDtypeStruct((batch, rows, seq), dt)
        seq_tile = lambda rows: pl.BlockSpec(
            (1, rows, tm), lambda i: (i // tiles_per_seq, 0, i % tiles_per_seq))
        args.append(w_t)
        in_specs.append(once(w_t))
        out_shape += [seq_major(D_MODEL, F32), seq_major(D_MODEL, F32), seq_major(N_HEADS, F32),
                      seq_major(KEY_EXT, BF16), seq_major(KEY_EXT, BF16)]
        out_specs += [seq_tile(D_MODEL), seq_tile(D_MODEL), seq_tile(N_HEADS),
                      seq_tile(KEY_EXT), seq_tile(KEY_EXT)]
        scratch = [pltpu.VMEM((N_HEADS, 1), F32)]
    else:
        tiles_per_seq = 1
        out_shape += [big(F32), big(F32), jax.ShapeDtypeStruct((n, N_HEADS), F32),
                      jax.ShapeDtypeStruct((N_HEADS, n), F32)]
        out_specs += [tile, tile, pl.BlockSpec((tm, N_HEADS), row),
                      pl.BlockSpec((N_HEADS, tm), lambda i: (0, i))]
    body = functools.partial(_inproj_kernel, emit_keys=emit_keys, tiles_per_seq=tiles_per_seq)
    return pl.pallas_call(body, grid=(n // tm,), in_specs=in_specs, out_specs=out_specs,
                          out_shape=out_shape, scratch_shapes=scratch,
                          compiler_params=_params("arbitrary"), name="in_projection")(*args)


def _flash_kernel(q_ref, ke_ref, ve_ref, o_ref):
    t = ATTN_TILE
    i = pl.program_id(2)
    lane = lax.broadcasted_iota(jnp.int32, (1, LANES), 1)
    first = lane < HEAD_DIM
    ones_at = lambda cond: jnp.where(cond, 1.0, 0.0).astype(BF16)
    q2 = q_ref[...]
    q_ext = (jnp.where(first, q2, ones_at(lane < HEAD_DIM + N_BIAS)),
             jnp.where(first, ones_at(lane < N_BIAS), q2))
    den_lane = (HEAD_DIM, 0)
    nt = (((1,), (1,)), ((), ()))

    def step(j, carry, masked):
        ks = pl.multiple_of(j * t, t)
        out = []
        for h in range(HEADS_PER_LANE_TILE):
            m, acc = carry[2 * h], carry[2 * h + 1]
            rows = slice(h * LANES, (h + 1) * LANES)
            s = _dot(q_ext[h], ke_ref[0, rows, pl.ds(ks, t)])
            if masked:
                rr = lax.broadcasted_iota(jnp.int32, (t, t), 0)
                cc = lax.broadcasted_iota(jnp.int32, (t, t), 1)
                s = jnp.where(cc <= rr, s, -jnp.inf)
            m_new = jnp.maximum(m, jnp.max(s, axis=-1, keepdims=True))
            alpha = jnp.exp2(m - m_new)
            p = jnp.exp2(s - m_new).astype(BF16)
            pv = lax.dot_general(p, ve_ref[0, rows, pl.ds(ks, t)], nt, preferred_element_type=F32)
            out += [m_new, alpha * acc + pv]
        return tuple(out)

    neg = jnp.full((t, 1), -jnp.inf, F32)
    za = jnp.zeros((t, LANES), F32)
    carry = lax.fori_loop(0, i, lambda j, c: step(j, c, False), (neg, za, neg, za))
    _, acc0, _, acc1 = step(i, carry, True)
    inv0 = 1.0 / acc0[:, den_lane[0]:den_lane[0] + 1]
    inv1 = 1.0 / acc1[:, den_lane[1]:den_lane[1] + 1]
    o_ref[...] = jnp.where(first, acc0 * inv0, acc1 * inv1).astype(o_ref.dtype)


def _prompt_attention(qb, ke, ve, batch, seq):
    t = ATTN_TILE
    nq = seq // t
    pairs = N_HEADS // HEADS_PER_LANE_TILE
    q_spec = pl.BlockSpec((t, LANES), lambda b, hp, i: (b * nq + i, hp))
    e_spec = pl.BlockSpec((1, HEADS_PER_LANE_TILE * LANES, seq), lambda b, hp, i: (b, hp, 0))
    return pl.pallas_call(_flash_kernel, grid=(batch, pairs, nq),
                          in_specs=[q_spec, e_spec, e_spec], out_specs=q_spec,
                          out_shape=jax.ShapeDtypeStruct(qb.shape, BF16),
                          compiler_params=_params("parallel", "parallel", "arbitrary"),
                          name="prompt_attention")(qb, ke, ve)


def _paged_kernel(pt_ref, *refs):
    pp = PAGES_PER_STEP
    k_refs, v_refs, lf_refs = refs[:pp], refs[pp:2 * pp], refs[2 * pp:3 * pp]
    qrep_ref, q_ref, kn_ref, vn_ref, lfnt_ref, o_ref, m_ref, l_ref, c_ref, acc_ref = refs[3 * pp:]
    r = pl.program_id(0)
    g = pl.program_id(1)
    wide = (N_HEADS, PAGE_SIZE)

    @pl.when(g == 0)
    def _():
        m_ref[...] = jnp.full(m_ref.shape, -jnp.inf, F32)
        l_ref[...] = jnp.zeros(l_ref.shape, F32)
        c_ref[...] = jnp.zeros(c_ref.shape, F32)
        acc_ref[...] = jnp.zeros(acc_ref.shape, F32)

    upper = _upper_ones(PAGE_SIZE)
    carry = c_ref[...]
    logits = []
    for p_i in range(pp):
        incl = _cumsum_lanes(lf_refs[p_i][0] * LOG2E, upper, carry)
        carry = jnp.broadcast_to(incl[:, PAGE_SIZE - 1:PAGE_SIZE], wide)
        qk = [jnp.sum(k_refs[p_i][0, h] * qrep_ref[0, h], axis=0, keepdims=True) for h in range(N_HEADS)]
        logits.append(jnp.concatenate(qk, axis=0) - incl)
    c_ref[...] = carry
    m_old = m_ref[...]
    step_max = jnp.max(_tree(logits, jnp.maximum), axis=-1, keepdims=True)
    m_new = jnp.maximum(m_old, jnp.broadcast_to(step_max, wide))
    m_ref[...] = m_new
    alpha = jnp.exp2(m_old - m_new)
    probs = [jnp.exp2(lg - m_new) for lg in logits]
    l_ref[...] = alpha * l_ref[...] + _tree(probs, jnp.add)
    for h in range(N_HEADS):
        head = slice(h, h + 1)
        pv = _tree([probs[p_i][head, :] * v_refs[p_i][0, h] for p_i in range(pp)], jnp.add)
        acc_ref[h] = acc_ref[h] * alpha[head, :] + pv

    @pl.when(g == pl.num_programs(1) - 1)
    def _():
        q = q_ref[0].astype(F32)
        s_new = jnp.sum(kn_ref[0] * q, axis=-1, keepdims=True)
        lane = lax.broadcasted_iota(jnp.int32, lfnt_ref.shape, 1)
        lf_new = jnp.sum(jnp.where(lane == r, lfnt_ref[...], 0.0), axis=-1, keepdims=True)
        logit = s_new - (c_ref[:, 0:1] + lf_new * LOG2E)
        m_last = m_ref[:, 0:1]
        m_fin = jnp.maximum(m_last, logit)
        a_fin = jnp.exp2(m_last - m_fin)
        p_new = jnp.exp2(logit - m_fin)
        l_fin = a_fin * jnp.sum(l_ref[...], axis=-1, keepdims=True) + p_new
        for h in range(N_HEADS):
            head = slice(h, h + 1)
            tot = jnp.sum(acc_ref[h], axis=-1, keepdims=True)
            o_ref[0, h] = (a_fin[head] * tot + p_new[head] * vn_ref[0, h]) / l_fin[head]


def _sample_attention(q_s, k_s, v_s, lft_s, cache_kt, cache_vt, cache_lft, page_table):
    n_req, n_pages = page_table.shape
    pp = PAGES_PER_STEP
    steps = n_pages // pp
    q_rep = jnp.broadcast_to(q_s.astype(F32)[..., None], q_s.shape + (PAGE_SIZE,))

    def page_map(p_i, nd):
        return lambda r, g, pt: (pt[r * n_pages + g * pp + p_i],) + (0,) * nd

    k_specs = [pl.BlockSpec((1, N_HEADS, HEAD_DIM, PAGE_SIZE), page_map(p, 3)) for p in range(pp)]
    lf_specs = [pl.BlockSpec((1, N_HEADS, PAGE_SIZE), page_map(p, 2)) for p in range(pp)]
    req3 = lambda r, g, pt: (r, 0, 0)
    req4 = lambda r, g, pt: (r, 0, 0, 0)
    head = pl.BlockSpec((1, N_HEADS, HEAD_DIM), req3)
    column = pl.BlockSpec((1, N_HEADS, HEAD_DIM, 1), req4)
    wide = lambda: pltpu.VMEM((N_HEADS, PAGE_SIZE), F32)
    grid_spec = pltpu.PrefetchScalarGridSpec(
        num_scalar_prefetch=1, grid=(n_req, steps),
        in_specs=k_specs + k_specs + lf_specs
        + [pl.BlockSpec((1, N_HEADS, HEAD_DIM, PAGE_SIZE), req4), head, head, column,
           pl.BlockSpec(lft_s.shape, lambda r, g, pt: (0, 0))],
        out_specs=column,
        scratch_shapes=[wide(), wide(), wide(), pltpu.VMEM((N_HEADS, HEAD_DIM, PAGE_SIZE), F32)])
    args = [cache_kt] * pp + [cache_vt] * pp + [cache_lft] * pp
    args += [q_rep, q_s, k_s, v_s[..., None], lft_s]
    out = pl.pallas_call(_paged_kernel, grid_spec=grid_spec,
                         out_shape=jax.ShapeDtypeStruct((n_req, N_HEADS, HEAD_DIM, 1), F32),
                         compiler_params=_params("arbitrary", "arbitrary"),
                         name="sample_attention")(page_table.reshape(-1), *args)
    return out.reshape(n_req, N_HEADS * HEAD_DIM).astype(BF16)


def _conv_prompt_kernel(u_ref, w_ref, b_ref, g_ref, bn_ref, o_ref, buf_ref, y_ref):
    ts = u_ref.shape[0]
    halo = 32
    j = pl.program_id(1)

    @pl.when(j == 0)
    def _():
        buf_ref[0:halo, :] = jnp.zeros((halo, D_MODEL), F32)

    buf_ref[halo:halo + ts, :] = u_ref[...]
    for c in range(D_MODEL // LANES):
        sl = slice(c * LANES, (c + 1) * LANES)
        acc = jnp.zeros((ts, LANES), F32)
        for tap in range(CONV_WIDTH):
            off = halo - (CONV_WIDTH - 1) + tap
            acc = acc + buf_ref[off:off + ts, sl] * w_ref[tap:tap + 1, sl]
        y_ref[:, sl] = acc + b_ref[:, sl]
    buf_ref[0:halo, :] = buf_ref[ts:ts + halo, :]
    y = _layer_norm(y_ref[...], g_ref[...], bn_ref[...])
    o_ref[...] = (y * _sigmoid(y)).astype(o_ref.dtype)


def _conv_prompt(u, conv_w, conv_b, g, bn, batch, seq):
    ts = TOKEN_TILE
    ns = seq // ts
    tile = pl.BlockSpec((ts, D_MODEL), lambda b, j: (b * ns + j, 0))
    const = lambda a: pl.BlockSpec(a.shape, lambda b, j: (0, 0))
    return pl.pallas_call(_conv_prompt_kernel, grid=(batch, ns),
                          in_specs=[tile, const(conv_w), const(conv_b), const(g), const(bn)],
                          out_specs=tile, out_shape=jax.ShapeDtypeStruct(u.shape, BF16),
                          scratch_shapes=[pltpu.VMEM((32 + ts, D_MODEL), F32),
                                          pltpu.VMEM((ts, D_MODEL), F32)],
                          compiler_params=_params("parallel", "arbitrary"),
                          name="conv_prompt")(u, conv_w, conv_b, g, bn)


def _conv_sample_kernel(state_ref, u_ref, w_ref, b_ref, g_ref, bn_ref, o_ref):
    hist = CONV_WIDTH - 1
    y = u_ref[...] * w_ref[hist:hist + 1, :] + b_ref[...]
    for tap in range(hist):
        y = y + state_ref[tap] * w_ref[tap:tap + 1, :]
    y = _layer_norm(y, g_ref[...], bn_ref[...])
    o_ref[...] = (y * _sigmoid(y)).astype(o_ref.dtype)


def _conv_sample(state_t, u, conv_w, conv_b, g, bn):
    full = lambda a: pl.BlockSpec(a.shape, lambda i: (0,) * a.ndim)
    args = (state_t, u, conv_w, conv_b, g, bn)
    return pl.pallas_call(_conv_sample_kernel, grid=(1,), in_specs=[full(a) for a in args],
                          out_specs=full(u), out_shape=jax.ShapeDtypeStruct(u.shape, BF16),
                          compiler_params=_params("arbitrary"), name="conv_sample")(*args)


def _merge_router_kernel(att_ref, conv_ref, ga_ref, gc_ref, x_ref, wa_ref, wc_ref, wo_ref,
                         g1_ref, b1_ref, wr_ref, br_ref, cnt_in_ref,
                         h_ref, idx_ref, wts_ref, rank_ref, cnt_ref):
    tm = x_ref.shape[0]

    @pl.when(pl.program_id(0) == 0)
    def _():
        cnt_ref[...] = cnt_in_ref[...]

    a = _dot(att_ref[...], wa_ref[...])
    c = _dot(conv_ref[...], wc_ref[...])
    mixed = (ga_ref[...] * a + gc_ref[...] * c).astype(BF16)
    res = DEEPNORM_ALPHA * x_ref[...] + _dot(mixed, wo_ref[...])
    h = _layer_norm(res, g1_ref[...], b1_ref[...])
    h_ref[...] = h

    hh, hm, hl = _split3(h)
    w_hi, w_mid, w_lo = wr_ref[0], wr_ref[1], wr_ref[2]
    logits = (_dot(hh, w_hi) + _dot(hh, w_mid) + _dot(hm, w_hi)
              + _dot(hh, w_lo) + _dot(hm, w_mid) + _dot(hl, w_hi)) + br_ref[...]

    eid = lax.broadcasted_iota(jnp.int32, (tm, N_EXPERTS), 1).astype(F32)
    k_lane = lax.broadcasted_iota(jnp.int32, (tm, TOP_K), 1)
    remaining = logits
    chosen = jnp.zeros((tm, N_EXPERTS), F32)
    vals, picks = [], []
    for _ in range(TOP_K):
        mx = jnp.max(remaining, axis=-1, keepdims=True)
        pick = jnp.min(jnp.where(remaining == mx, eid, N_EXPERTS), axis=-1, keepdims=True)
        hit = eid == pick
        chosen = jnp.where(hit, 1.0, chosen)
        remaining = jnp.where(hit, -jnp.inf, remaining)
        vals.append(mx)
        picks.append(pick)
    exps = [jnp.exp(v - vals[0]) for v in vals]
    denom = exps[0] + exps[1] + exps[2] + exps[3]

    rr = lax.broadcasted_iota(jnp.int32, (tm, tm), 0)
    cc = lax.broadcasted_iota(jnp.int32, (tm, tm), 1)
    strict_lower = jnp.where(cc < rr, 1.0, 0.0).astype(BF16)
    rank_dense = _dot(strict_lower, chosen.astype(BF16)) + cnt_ref[...]

    idx_out = jnp.zeros((tm, TOP_K), F32)
    wts_out = jnp.zeros((tm, TOP_K), F32)
    rank_out = jnp.zeros((tm, TOP_K), F32)
    for k in range(TOP_K):
        rk = jnp.sum(jnp.where(eid == picks[k], rank_dense, 0.0), axis=-1, keepdims=True)
        idx_out = jnp.where(k_lane == k, picks[k], idx_out)
        wts_out = jnp.where(k_lane == k, exps[k] / denom, wts_out)
        rank_out = jnp.where(k_lane == k, rk, rank_out)
    idx_ref[...] = idx_out.astype(jnp.int32)
    wts_ref[...] = wts_out
    rank_ref[...] = rank_out.astype(jnp.int32)
    cnt_ref[...] = cnt_ref[...] + jnp.sum(chosen, axis=0, keepdims=True)


def _merge_router(att, conv, ga, gc, x, wa, wc, wo, g1, b1, wr3, br, cnt_in, tm):
    n = x.shape[0]
    row = lambda i: (i, 0)
    tile = pl.BlockSpec((tm, D_MODEL), row)
    small = pl.BlockSpec((tm, TOP_K), row)
    const = lambda a: pl.BlockSpec(a.shape, lambda i: (0,) * a.ndim)
    out_shape = (jax.ShapeDtypeStruct((n, D_MODEL), F32),
                 jax.ShapeDtypeStruct((n, TOP_K), jnp.int32),
                 jax.ShapeDtypeStruct((n, TOP_K), F32),
                 jax.ShapeDtypeStruct((n, TOP_K), jnp.int32),
                 jax.ShapeDtypeStruct((1, N_EXPERTS), F32))
    return pl.pallas_call(
        _merge_router_kernel, grid=(n // tm,),
        in_specs=[tile, tile, tile, tile, tile, const(wa), const(wc), const(wo),
                  const(g1), const(b1), const(wr3), const(br), const(cnt_in)],
        out_specs=(tile, small, small, small, const(cnt_in)), out_shape=out_shape,
        compiler_params=_params("arbitrary"), name="merge_router",
    )(att, conv, ga, gc, x, wa, wc, wo, g1, b1, wr3, br, cnt_in)


def _row_copy(src_ref, src_row, dst_ref, dst_row, sem):
    return pltpu.make_async_copy(src_ref.at[pl.ds(src_row, 1), :], dst_ref.at[pl.ds(dst_row, 1), :], sem)


def _dispatch_kernel(pos_ref, h_ref, xs_in_ref, xs_ref, sem):
    del xs_in_ref
    tm = h_ref.shape[0]

    def issue(t, carry):
        for k in range(TOP_K):
            _row_copy(h_ref, t, xs_ref, pos_ref[0, 0, t * TOP_K + k], sem).start()
        return carry

    lax.fori_loop(0, tm, issue, 0)
    for _ in range(TOP_K):
        pltpu.make_async_copy(h_ref, xs_ref.at[pl.ds(0, tm), :], sem).wait()


def _dispatch(pos, h, xs, tm):
    n = h.shape[0]
    pos3 = pos.reshape(n // tm, 1, tm * TOP_K)
    return pl.pallas_call(
        _dispatch_kernel, grid=(n // tm,),
        in_specs=[pl.BlockSpec((1, 1, tm * TOP_K), lambda i: (i, 0, 0), memory_space=pltpu.SMEM),
                  pl.BlockSpec((tm, D_MODEL), lambda i: (i, 0)),
                  pl.BlockSpec(memory_space=pl.ANY)],
        out_specs=pl.BlockSpec(memory_space=pl.ANY),
        out_shape=jax.ShapeDtypeStruct(xs.shape, xs.dtype),
        scratch_shapes=[pltpu.SemaphoreType.DMA(())],
        input_output_aliases={2: 0},
        compiler_params=_params("arbitrary"), name="moe_dispatch")(pos3, h, xs)


def _experts_kernel(te_ref, tv_ref, xs_ref, wgu_ref, bgu_ref, wd_ref, bd_ref, o_ref, wgu_bf, wd_bf):
    n = pl.program_id(0)
    prev = te_ref[jnp.maximum(n - 1, 0)]

    @pl.when((n == 0) | (te_ref[n] != prev))
    def _():
        wgu_bf[...] = wgu_ref[0].astype(BF16)
        wd_bf[...] = wd_ref[0].astype(BF16)

    @pl.when(tv_ref[n] == 1)
    def _():
        gu = _dot(xs_ref[...].astype(BF16), wgu_bf[...]) + bgu_ref[0]
        gate = jnp.minimum(gu[:, :D_FF], SWIGLU_LIMIT)
        up = jnp.clip(gu[:, D_FF:], -SWIGLU_LIMIT, SWIGLU_LIMIT)
        act = (up + 1.0) * gate * _sigmoid(SWIGLU_ALPHA * gate)
        o_ref[...] = _dot(act.astype(BF16), wd_bf[...]) + bd_ref[0]

    @pl.when(tv_ref[n] == 0)
    def _():
        o_ref[...] = jnp.zeros(o_ref.shape, F32)


def _experts(tile_expert, tile_valid, xs, w_gate_up, b_gate_up, w_down, b_down):
    rows = xs.shape[0]
    tm = ROW_TILE
    e3 = lambda n, te, tv: (te[n], 0, 0)
    row = lambda n, te, tv: (n, 0)
    grid_spec = pltpu.PrefetchScalarGridSpec(
        num_scalar_prefetch=2, grid=(rows // tm,),
        in_specs=[pl.BlockSpec((tm, D_MODEL), row),
                  pl.BlockSpec((1, D_MODEL, 2 * D_FF), e3),
                  pl.BlockSpec((1, 1, 2 * D_FF), e3),
                  pl.BlockSpec((1, D_FF, D_MODEL), e3),
                  pl.BlockSpec((1, 1, D_MODEL), e3)],
        out_specs=pl.BlockSpec((tm, D_MODEL), row),
        scratch_shapes=[pltpu.VMEM((D_MODEL, 2 * D_FF), BF16), pltpu.VMEM((D_FF, D_MODEL), BF16)])
    return pl.pallas_call(_experts_kernel, grid_spec=grid_spec,
                          out_shape=jax.ShapeDtypeStruct((rows, D_MODEL), F32),
                          compiler_params=_params("arbitrary"), name="moe_experts",
                          )(tile_expert, tile_valid, xs, w_gate_up,
                            b_gate_up.reshape(N_EXPERTS, 1, 2 * D_FF), w_down,
                            b_down.reshape(N_EXPERTS, 1, D_MODEL))


def _combine_kernel(pos_ref, wts_ref, h_ref, ys_ref, g_ref, b_ref, o_ref, buf_ref, sem):
    tm = h_ref.shape[0]

    def issue(t, carry):
        for k in range(TOP_K):
            _row_copy(ys_ref, pos_ref[0, 0, t * TOP_K + k], buf_ref.at[k], t, sem).start()
        return carry

    lax.fori_loop(0, tm, issue, 0)
    for k in range(TOP_K):
        pltpu.make_async_copy(ys_ref.at[pl.ds(0, tm), :], buf_ref.at[k], sem).wait()
    wts = wts_ref[...]
    moe = wts[:, 0:1] * buf_ref[0]
    for k in range(1, TOP_K):
        moe = moe + wts[:, k:k + 1] * buf_ref[k]
    o_ref[...] = _layer_norm(DEEPNORM_ALPHA * h_ref[...] + moe, g_ref[...], b_ref[...])


def _combine(pos, wts, h, ys, g2, b2, tm):
    n = h.shape[0]
    pos3 = pos.reshape(n // tm, 1, tm * TOP_K)
    row = lambda i: (i, 0)
    const = lambda a: pl.BlockSpec(a.shape, lambda i: (0,) * a.ndim)
    return pl.pallas_call(
        _combine_kernel, grid=(n // tm,),
        in_specs=[pl.BlockSpec((1, 1, tm * TOP_K), lambda i: (i, 0, 0), memory_space=pltpu.SMEM),
                  pl.BlockSpec((tm, TOP_K), row),
                  pl.BlockSpec((tm, D_MODEL), row),
                  pl.BlockSpec(memory_space=pl.ANY), const(g2), const(b2)],
        out_specs=pl.BlockSpec((tm, D_MODEL), row),
        out_shape=jax.ShapeDtypeStruct((n, D_MODEL), F32),
        scratch_shapes=[pltpu.VMEM((TOP_K, tm, D_MODEL), F32), pltpu.SemaphoreType.DMA(())],
        compiler_params=_params("arbitrary"), name="moe_combine")(pos3, wts, h, ys, g2, b2)


def _split_in_proj(w_in, b_forget):
    a = N_HEADS * HEAD_DIM
    cuts = [0, a, 2 * a, 3 * a]
    f0 = 3 * a
    rest = f0 + N_HEADS
    starts = cuts[:3] + [rest + i * D_MODEL for i in range(4)]
    w_main = jnp.stack([w_in[:, s:s + D_MODEL] for s in starts]).astype(BF16)
    w_t = jnp.stack([w_in[:, s:s + a].T for s in cuts[1:3]]).astype(BF16)
    w_f = w_in[:, f0:rest].astype(BF16)
    return w_main, w_t, w_f, w_f.T, b_forget.reshape(1, N_HEADS), b_forget.reshape(N_HEADS, 1)


def _routing_tables(counts, n_tiles):
    cnt = counts.reshape(N_EXPERTS).astype(jnp.int32)
    tiles = (cnt + ROW_TILE - 1) // ROW_TILE
    tile_end = jnp.cumsum(tiles)
    start_row = (tile_end - tiles) * ROW_TILE
    n = jnp.arange(n_tiles, dtype=jnp.int32)
    valid = n < tile_end[-1]
    owner = jnp.sum((n[:, None] >= tile_end[None, :]).astype(jnp.int32), axis=1)
    last_owner = jnp.sum((tile_end[-1] - 1 >= tile_end).astype(jnp.int32))
    tile_expert = jnp.where(valid, owner, last_owner).astype(jnp.int32)
    return start_row, tile_expert, valid.astype(jnp.int32)


def _layer(xp, xs, cache_k, cache_v, cache_logf, state_conv, page_table,
           w_in, b_forget, conv_w, conv_b, conv_norm_g, conv_norm_b,
           w_attn_proj, w_conv_proj, w_out, ln1_g, ln1_b,
           w_router, b_router, w_gate_up, b_gate_up, w_down, b_down, ln2_g, ln2_b):
    batch, seq, d = xp.shape
    n_req = xs.shape[0]
    n_p = batch * seq
    row = lambda a: a.reshape(1, -1)
    heads = lambda a: a.reshape(a.shape[0], N_HEADS, HEAD_DIM)

    w_main, w_t, w_f, w_ft, b_f, b_ft = _split_in_proj(w_in, b_forget)
    wa, wc, wo = (w.astype(BF16) for w in (w_attn_proj, w_conv_proj, w_out))
    wr3 = jnp.stack(_split3(w_router))
    cw, cb, cg, cbn = conv_w, row(conv_b), row(conv_norm_g), row(conv_norm_b)
    g1, b1, g2, b2, br = row(ln1_g), row(ln1_b), row(ln2_g), row(ln2_b), row(b_router)

    xp2 = xp.reshape(n_p, d)
    qb, u_p, ga_p, gc_p, kt_p, vt_p, lft_p, ke, ve = _in_projection(
        xp2, w_main, w_f, w_ft, b_f, b_ft, TOKEN_TILE, w_t, seq)
    att_p = _prompt_attention(qb, ke, ve, batch, seq)
    conv_p = _conv_prompt(u_p, cw, cb, cg, cbn, batch, seq)

    xs2 = xs.reshape(n_req, d)
    q_s, u_s, ga_s, gc_s, k_s, v_s, lf_s, lft_s = _in_projection(
        xs2, w_main, w_f, w_ft, b_f, b_ft, n_req)
    att_s = _sample_attention(heads(q_s), heads(k_s), heads(v_s), lft_s,
                              jnp.transpose(cache_k, (0, 2, 3, 1)), jnp.transpose(cache_v, (0, 2, 3, 1)),
                              jnp.transpose(cache_logf, (0, 2, 1)), page_table)
    state_t = jnp.transpose(state_conv, (1, 0, 2))
    conv_s = _conv_sample(state_t, u_s, cw, cb, cg, cbn)

    zero_cnt = jnp.zeros((1, N_EXPERTS), F32)
    h_p, idx_p, wts_p, rank_p, cnt_p = _merge_router(
        att_p, conv_p, ga_p, gc_p, xp2, wa, wc, wo, g1, b1, wr3, br, zero_cnt, TOKEN_TILE)
    h_s, idx_s, wts_s, rank_s, cnt = _merge_router(
        att_s, conv_s, ga_s, gc_s, xs2, wa, wc, wo, g1, b1, wr3, br, cnt_p, n_req)

    n_tok = n_p + n_req
    n_tiles = (n_tok * TOP_K + N_EXPERTS * (ROW_TILE - 1) + ROW_TILE - 1) // ROW_TILE
    start_row, tile_expert, tile_valid = _routing_tables(cnt, n_tiles)
    pos_p = start_row[idx_p] + rank_p
    pos_s = start_row[idx_s] + rank_s
    sorted_rows = jnp.zeros((n_tiles * ROW_TILE, d), F32)
    sorted_rows = _dispatch(pos_p, h_p, sorted_rows, TOKEN_TILE)
    sorted_rows = _dispatch(pos_s, h_s, sorted_rows, n_req)
    expert_out = _experts(tile_expert, tile_valid, sorted_rows, w_gate_up, b_gate_up, w_down, b_down)
    y_p = _combine(pos_p, wts_p, h_p, expert_out, g2, b2, TOKEN_TILE)
    y_s = _combine(pos_s, wts_s, h_s, expert_out, g2, b2, n_req)

    hist = CONV_WIDTH - 1
    token_major = lambda t: jnp.transpose(t.reshape(batch, N_HEADS, HEAD_DIM, seq), (0, 3, 1, 2))
    conv_state_p = u_p.reshape(batch, seq, d)[:, seq - hist:, :]
    conv_state_s = jnp.transpose(jnp.concatenate([state_t[1:], u_s[None]], axis=0), (1, 0, 2))
    return (y_p.reshape(batch, seq, d), y_s.reshape(n_req, 1, d),
            token_major(kt_p), token_major(vt_p), jnp.transpose(lft_p, (0, 2, 1)), conv_state_p,
            k_s.reshape(n_req, 1, N_HEADS, HEAD_DIM), v_s.reshape(n_req, 1, N_HEADS, HEAD_DIM),
            lf_s.reshape(n_req, 1, N_HEADS), conv_state_s)


def kernel(x_prompt, x_sample, cache_k, cache_v, cache_logf, state_conv, page_table, w_in, b_forget, conv_w, conv_b, conv_norm_g, conv_norm_b, w_attn_proj, w_conv_proj, w_out, ln1_g, ln1_b, w_router, b_router, w_gate_up, b_gate_up, w_down, b_down, ln2_g, ln2_b):
    assert x_prompt.shape[-1] == D_MODEL and w_in.shape[0] == DEPTH
    out = _layer(x_prompt, x_sample, cache_k[0], cache_v[0], cache_logf[0], state_conv[0], page_table,
                 w_in[0], b_forget[0], conv_w[0], conv_b[0], conv_norm_g[0], conv_norm_b[0],
                 w_attn_proj[0], w_conv_proj[0], w_out[0], ln1_g[0], ln1_b[0],
                 w_router[0], b_router[0], w_gate_up[0], b_gate_up[0], w_down[0], b_down[0],
                 ln2_g[0], ln2_b[0])
    y_p, y_s = out[0], out[1]
    return (y_p, y_s) + tuple(o[None] for o in out[2:])
```

```python
import functools
import math

import jax
import jax.numpy as jnp
from jax import lax
from jax.experimental import pallas as pl
from jax.experimental.pallas import tpu as pltpu

F32 = jnp.float32
BF16 = jnp.bfloat16

D_MODEL = 1024
N_HEADS = 16
HEAD_DIM = 64
CONV_WIDTH = 31
N_EXPERTS = 32
TOP_K = 4
D_FF = 1024
PAGE_SIZE = 128
SWIGLU_ALPHA = 1.702
SWIGLU_LIMIT = 7.0
LN_EPS = 1e-5
DEPTH = 1
DEEPNORM_ALPHA = (2 * DEPTH) ** 0.25
LOG2E = math.log2(math.e)

LANES = 128
SUBLANES = 8
HEADS_PER_LANE_TILE = LANES // HEAD_DIM
VMEM_LIMIT = 56 * 2 ** 20

TOKEN_TILE = 256
ATTN_TILE = 512
ROW_TILE = 256
PAGES_PER_STEP = 8
N_BIAS = 3
KEY_EXT = N_HEADS * LANES


def _params(*sem):
    return pltpu.CompilerParams(dimension_semantics=sem, vmem_limit_bytes=VMEM_LIMIT)


def _split3(x):
    hi = x.astype(BF16)
    r = x - hi.astype(F32)
    mid = r.astype(BF16)
    lo = (r - mid.astype(F32)).astype(BF16)
    return hi, mid, lo


def _dot(a, b):
    return jnp.dot(a, b, preferred_element_type=F32)


def _cumsum_lanes(x, upper, carry):
    hi, mid, lo = _split3(x)
    return _dot(hi, upper) + _dot(mid, upper) + _dot(lo, upper) + carry


def _upper_ones(n):
    rr = lax.broadcasted_iota(jnp.int32, (n, n), 0)
    cc = lax.broadcasted_iota(jnp.int32, (n, n), 1)
    return jnp.where(rr <= cc, 1.0, 0.0).astype(BF16)


def _log_sigmoid(x):
    return jnp.minimum(x, 0.0) - jnp.log1p(jnp.exp(-jnp.abs(x)))


def _sigmoid(x):
    return 1.0 / (1.0 + jnp.exp(-x))


def _layer_norm(x, g, b):
    mu = jnp.mean(x, axis=-1, keepdims=True)
    xc = x - mu
    var = jnp.mean(xc * xc, axis=-1, keepdims=True)
    return xc * lax.rsqrt(var + LN_EPS) * g + b


def _tree(xs, op):
    xs = list(xs)
    while len(xs) > 1:
        xs = [op(xs[i], xs[i + 1]) if i + 1 < len(xs) else xs[i] for i in range(0, len(xs), 2)]
    return xs[0]


def _extended_rows(rows, extra, head):
    pad = jnp.zeros((HEAD_DIM - SUBLANES, rows.shape[1]), F32)
    return [rows, extra, pad] if head % 2 == 0 else [extra, pad, rows]


def _inproj_kernel(*refs, emit_keys, tiles_per_seq):
    if emit_keys:
        (x_ref, w_ref, wf_ref, wft_ref, bf_ref, bft_ref, wt_ref,
         q_ref, u_ref, ga_ref, gc_ref, kt_ref, vt_ref, lft_ref, ke_ref, ve_ref, carry_ref) = refs
    else:
        (x_ref, w_ref, wf_ref, wft_ref, bf_ref, bft_ref,
         q_ref, u_ref, ga_ref, gc_ref, k_ref, v_ref, lf_ref, lft_ref) = refs
    tm = x_ref.shape[0]
    nt = (((1,), (1,)), ((), ()))
    x = x_ref[...].astype(BF16)
    q = _dot(x, w_ref[0])
    q_ref[...] = (q * (LOG2E * HEAD_DIM ** -0.5)).astype(BF16)
    u_ref[...] = _dot(x, w_ref[3]) * _sigmoid(_dot(x, w_ref[4]))
    ga_ref[...] = _sigmoid(_dot(x, w_ref[5]))
    gc_ref[...] = _sigmoid(_dot(x, w_ref[6]))
    ft = lax.dot_general(wft_ref[...], x, nt, preferred_element_type=F32)
    lft = _log_sigmoid(ft + bft_ref[...])
    if not emit_keys:
        k_ref[...] = _dot(x, w_ref[1])
        v_ref[...] = _dot(x, w_ref[2])
        lf_ref[...] = _log_sigmoid(_dot(x, wf_ref[...]) + bf_ref[...])
        lft_ref[...] = lft
        return

    kt = lax.dot_general(wt_ref[0], x, nt, preferred_element_type=F32)
    vt = lax.dot_general(wt_ref[1], x, nt, preferred_element_type=F32)
    kt_ref[0] = kt
    vt_ref[0] = vt
    lft_ref[0] = lft

    @pl.when(pl.program_id(0) % tiles_per_seq == 0)
    def _():
        carry_ref[...] = jnp.zeros(carry_ref.shape, F32)

    csum = _cumsum_lanes(lft * LOG2E, _upper_ones(tm), carry_ref[...])
    carry_ref[...] = csum[:, tm - 1:tm]
    pieces = [p.astype(F32) for p in _split3(-csum)]
    sub = lax.broadcasted_iota(jnp.int32, (SUBLANES, tm), 0)
    ones_rows = jnp.where(sub == 0, 1.0, 0.0)
    k_blocks, v_blocks = [], []
    for h in range(N_HEADS):
        bias_rows = jnp.zeros((SUBLANES, tm), F32)
        for i, piece in enumerate(pieces):
            bias_rows = jnp.where(sub == i, piece[h:h + 1, :], bias_rows)
        feat = slice(h * HEAD_DIM, (h + 1) * HEAD_DIM)
        k_blocks += _extended_rows(kt[feat], bias_rows, h)
        v_blocks += _extended_rows(vt[feat], ones_rows, h)
    ke_ref[0] = jnp.concatenate(k_blocks, axis=0).astype(BF16)
    ve_ref[0] = jnp.concatenate(v_blocks, axis=0).astype(BF16)


def _in_projection(x, w_main, w_f, w_ft, b_f, b_ft, tm, w_t=None, seq=None):
    emit_keys = seq is not None
    n = x.shape[0]
    row = lambda i: (i, 0)
    const = lambda a: pl.BlockSpec(a.shape, lambda i: (0,) * a.ndim)
    once = lambda a: pl.BlockSpec(a.shape, lambda i: (0,) * a.ndim, pipeline_mode=pl.Buffered(1))
    big = lambda dt: jax.ShapeDtypeStruct((n, D_MODEL), dt)
    tile = pl.BlockSpec((tm, D_MODEL), row)
    args = [x, w_main, w_f, w_ft, b_f, b_ft]
    in_specs = [tile, once(w_main), const(w_f), const(w_ft), const(b_f), const(b_ft)]
    out_shape = [big(BF16), big(F32), big(F32), big(F32)]
    out_specs = [tile, tile, tile, tile]
    scratch = []
    if emit_keys:
        tiles_per_seq = seq // tm
        batch = n // seq
        seq_major = lambda rows, dt: jax.ShapeDtypeStruct((batch, rows, seq), dt)
        seq_tile = lambda rows: pl.BlockSpec(
            (1, rows, tm), lambda i: (i // tiles_per_seq, 0, i % tiles_per_seq))
        args.append(w_t)
        in_specs.append(once(w_t))
        out_shape += [seq_major(D_MODEL, F32), seq_major(D_MODEL, F32), seq_major(N_HEADS, F32),
                      seq_major(KEY_EXT, BF16), seq_major(KEY_EXT, BF16)]
        out_specs += [seq_tile(D_MODEL), seq_tile(D_MODEL), seq_tile(N_HEADS),
                      seq_tile(KEY_EXT), seq_tile(KEY_EXT)]
        scratch = [pltpu.VMEM((N_HEADS, 1), F32)]
    else:
        tiles_per_seq = 1
        out_shape += [big(F32), big(F32), jax.ShapeDtypeStruct((n, N_HEADS), F32),
                      jax.ShapeDtypeStruct((N_HEADS, n), F32)]
        out_specs += [tile, tile, pl.BlockSpec((tm, N_HEADS), row),
                      pl.BlockSpec((N_HEADS, tm), lambda i: (0, i))]
    body = functools.partial(_inproj_kernel, emit_keys=emit_keys, tiles_per_seq=tiles_per_seq)
    return pl.pallas_call(body, grid=(n // tm,), in_specs=in_specs, out_specs=out_specs,
                          out_shape=out_shape, scratch_shapes=scratch,
                          compiler_params=_params("arbitrary"), name="in_projection")(*args)


def _flash_kernel(q_ref, ke_ref, ve_ref, o_ref):
    t = ATTN_TILE
    i = pl.program_id(2)
    lane = lax.broadcasted_iota(jnp.int32, (1, LANES), 1)
    first = lane < HEAD_DIM
    ones_at = lambda cond: jnp.where(cond, 1.0, 0.0).astype(BF16)
    q2 = q_ref[...]
    q_ext = (jnp.where(first, q2, ones_at(lane < HEAD_DIM + N_BIAS)),
             jnp.where(first, ones_at(lane < N_BIAS), q2))
    den_lane = (HEAD_DIM, 0)
    nt = (((1,), (1,)), ((), ()))

    def step(j, carry, masked):
        ks = pl.multiple_of(j * t, t)
        out = []
        for h in range(HEADS_PER_LANE_TILE):
            m, acc = carry[2 * h], carry[2 * h + 1]
            rows = slice(h * LANES, (h + 1) * LANES)
            s = _dot(q_ext[h], ke_ref[0, rows, pl.ds(ks, t)])
            if masked:
                rr = lax.broadcasted_iota(jnp.int32, (t, t), 0)
                cc = lax.broadcasted_iota(jnp.int32, (t, t), 1)
                s = jnp.where(cc <= rr, s, -jnp.inf)
            m_new = jnp.maximum(m, jnp.max(s, axis=-1, keepdims=True))
            alpha = jnp.exp2(m - m_new)
            p = jnp.exp2(s - m_new).astype(BF16)
            pv = lax.dot_general(p, ve_ref[0, rows, pl.ds(ks, t)], nt, preferred_element_type=F32)
            out += [m_new, alpha * acc + pv]
        return tuple(out)

    neg = jnp.full((t, 1), -jnp.inf, F32)
    za = jnp.zeros((t, LANES), F32)
    carry = lax.fori_loop(0, i, lambda j, c: step(j, c, False), (neg, za, neg, za))
    _, acc0, _, acc1 = step(i, carry, True)
    inv0 = 1.0 / acc0[:, den_lane[0]:den_lane[0] + 1]
    inv1 = 1.0 / acc1[:, den_lane[1]:den_lane[1] + 1]
    o_ref[...] = jnp.where(first, acc0 * inv0, acc1 * inv1).astype(o_ref.dtype)


def _prompt_attention(qb, ke, ve, batch, seq):
    t = ATTN_TILE
    nq = seq // t
    pairs = N_HEADS // HEADS_PER_LANE_TILE
    q_spec = pl.BlockSpec((t, LANES), lambda b, hp, i: (b * nq + i, hp))
    e_spec = pl.BlockSpec((1, HEADS_PER_LANE_TILE * LANES, seq), lambda b, hp, i: (b, hp, 0))
    return pl.pallas_call(_flash_kernel, grid=(batch, pairs, nq),
                          in_specs=[q_spec, e_spec, e_spec], out_specs=q_spec,
                          out_shape=jax.ShapeDtypeStruct(qb.shape, BF16),
                          compiler_params=_params("parallel", "parallel", "arbitrary"),
                          name="prompt_attention")(qb, ke, ve)


def _paged_kernel(pt_ref, *refs):
    pp = PAGES_PER_STEP
    k_refs, v_refs, lf_refs = refs[:pp], refs[pp:2 * pp], refs[2 * pp:3 * pp]
    qrep_ref, q_ref, kn_ref, vn_ref, lfnt_ref, o_ref, m_ref, l_ref, c_ref, acc_ref = refs[3 * pp:]
    r = pl.program_id(0)
    g = pl.program_id(1)
    wide = (N_HEADS, PAGE_SIZE)

    @pl.when(g == 0)
    def _():
        m_ref[...] = jnp.full(m_ref.shape, -jnp.inf, F32)
        l_ref[...] = jnp.zeros(l_ref.shape, F32)
        c_ref[...] = jnp.zeros(c_ref.shape, F32)
        acc_ref[...] = jnp.zeros(acc_ref.shape, F32)

    upper = _upper_ones(PAGE_SIZE)
    carry = c_ref[...]
    logits = []
    for p_i in range(pp):
        incl = _cumsum_lanes(lf_refs[p_i][0] * LOG2E, upper, carry)
        carry = jnp.broadcast_to(incl[:, PAGE_SIZE - 1:PAGE_SIZE], wide)
        qk = [jnp.sum(k_refs[p_i][0, h] * qrep_ref[0, h], axis=0, keepdims=True) for h in range(N_HEADS)]
        logits.append(jnp.concatenate(qk, axis=0) - incl)
    c_ref[...] = carry
    m_old = m_ref[...]
    step_max = jnp.max(_tree(logits, jnp.maximum), axis=-1, keepdims=True)
    m_new = jnp.maximum(m_old, jnp.broadcast_to(step_max, wide))
    m_ref[...] = m_new
    alpha = jnp.exp2(m_old - m_new)
    probs = [jnp.exp2(lg - m_new) for lg in logits]
    l_ref[...] = alpha * l_ref[...] + _tree(probs, jnp.add)
    for h in range(N_HEADS):
        head = slice(h, h + 1)
        pv = _tree([probs[p_i][head, :] * v_refs[p_i][0, h] for p_i in range(pp)], jnp.add)
        acc_ref[h] = acc_ref[h] * alpha[head, :] + pv

    @pl.when(g == pl.num_programs(1) - 1)
    def _():
        q = q_ref[0].astype(F32)
        s_new = jnp.sum(kn_ref[0] * q, axis=-1, keepdims=True)
        lane = lax.broadcasted_iota(jnp.int32, lfnt_ref.shape, 1)
        lf_new = jnp.sum(jnp.where(lane == r, lfnt_ref[...], 0.0), axis=-1, keepdims=True)
        logit = s_new - (c_ref[:, 0:1] + lf_new * LOG2E)
        m_last = m_ref[:, 0:1]
        m_fin = jnp.maximum(m_last, logit)
        a_fin = jnp.exp2(m_last - m_fin)
        p_new = jnp.exp2(logit - m_fin)
        l_fin = a_fin * jnp.sum(l_ref[...], axis=-1, keepdims=True) + p_new
        for h in range(N_HEADS):
            head = slice(h, h + 1)
            tot = jnp.sum(acc_ref[h], axis=-1, keepdims=True)
            o_ref[0, h] = (a_fin[head] * tot + p_new[head] * vn_ref[0, h]) / l_fin[head]


def _sample_attention(q_s, k_s, v_s, lft_s, cache_kt, cache_vt, cache_lft, page_table):
    n_req, n_pages = page_table.shape
    pp = PAGES_PER_STEP
    steps = n_pages // pp
    q_rep = jnp.broadcast_to(q_s.astype(F32)[..., None], q_s.shape + (PAGE_SIZE,))

    def page_map(p_i, nd):
        return lambda r, g, pt: (pt[r * n_pages + g * pp + p_i],) + (0,) * nd

    k_specs = [pl.BlockSpec((1, N_HEADS, HEAD_DIM, PAGE_SIZE), page_map(p, 3)) for p in range(pp)]
    lf_specs = [pl.BlockSpec((1, N_HEADS, PAGE_SIZE), page_map(p, 2)) for p in range(pp)]
    req3 = lambda r, g, pt: (r, 0, 0)
    req4 = lambda r, g, pt: (r, 0, 0, 0)
    head = pl.BlockSpec((1, N_HEADS, HEAD_DIM), req3)
    column = pl.BlockSpec((1, N_HEADS, HEAD_DIM, 1), req4)
    wide = lambda: pltpu.VMEM((N_HEADS, PAGE_SIZE), F32)
    grid_spec = pltpu.PrefetchScalarGridSpec(
        num_scalar_prefetch=1, grid=(n_req, steps),
        in_specs=k_specs + k_specs + lf_specs
        + [pl.BlockSpec((1, N_HEADS, HEAD_DIM, PAGE_SIZE), req4), head, head, column,
           pl.BlockSpec(lft_s.shape, lambda r, g, pt: (0, 0))],
        out_specs=column,
        scratch_shapes=[wide(), wide(), wide(), pltpu.VMEM((N_HEADS, HEAD_DIM, PAGE_SIZE), F32)])
    args = [cache_kt] * pp + [cache_vt] * pp + [cache_lft] * pp
    args += [q_rep, q_s, k_s, v_s[..., None], lft_s]
    out = pl.pallas_call(_paged_kernel, grid_spec=grid_spec,
                         out_shape=jax.ShapeDtypeStruct((n_req, N_HEADS, HEAD_DIM, 1), F32),
                         compiler_params=_params("arbitrary", "arbitrary"),
                         name="sample_attention")(page_table.reshape(-1), *args)
    return out.reshape(n_req, N_HEADS * HEAD_DIM).astype(BF16)


def _conv_prompt_kernel(u_ref, w_ref, b_ref, g_ref, bn_ref, o_ref, buf_ref, y_ref):
    ts = u_ref.shape[0]
    halo = 32
    j = pl.program_id(1)

    @pl.when(j == 0)
    def _():
        buf_ref[0:halo, :] = jnp.zeros((halo, D_MODEL), F32)

    buf_ref[halo:halo + ts, :] = u_ref[...]
    first_off = halo - (CONV_WIDTH - 1)
    for c in range(D_MODEL // LANES):
        sl = slice(c * LANES, (c + 1) * LANES)
        y = None
        for r in range(SUBLANES):
            rows = ts if r == 0 else ts + SUBLANES
            z = None
            for a in range((halo + SUBLANES) // SUBLANES):
                tap = a * SUBLANES + r - first_off
                if 0 <= tap < CONV_WIDTH:
                    term = buf_ref[a * SUBLANES:a * SUBLANES + rows, sl] * w_ref[tap:tap + 1, sl]
                    z = term if z is None else z + term
            part = z if r == 0 else z[r:r + ts]
            y = part if y is None else y + part
        y_ref[:, sl] = y + b_ref[:, sl]
    buf_ref[0:halo, :] = buf_ref[ts:ts + halo, :]
    y = _layer_norm(y_ref[...], g_ref[...], bn_ref[...])
    o_ref[...] = (y * _sigmoid(y)).astype(o_ref.dtype)


def _conv_prompt(u, conv_w, conv_b, g, bn, batch, seq):
    ts = TOKEN_TILE
    ns = seq // ts
    tile = pl.BlockSpec((ts, D_MODEL), lambda b, j: (b * ns + j, 0))
    const = lambda a: pl.BlockSpec(a.shape, lambda b, j: (0, 0))
    return pl.pallas_call(_conv_prompt_kernel, grid=(batch, ns),
                          in_specs=[tile, const(conv_w), const(conv_b), const(g), const(bn)],
                          out_specs=tile, out_shape=jax.ShapeDtypeStruct(u.shape, BF16),
                          scratch_shapes=[pltpu.VMEM((32 + ts, D_MODEL), F32),
                                          pltpu.VMEM((ts, D_MODEL), F32)],
                          compiler_params=_params("parallel", "arbitrary"),
                          name="conv_prompt")(u, conv_w, conv_b, g, bn)


def _conv_sample_kernel(state_ref, u_ref, w_ref, b_ref, g_ref, bn_ref, o_ref):
    hist = CONV_WIDTH - 1
    y = u_ref[...] * w_ref[hist:hist + 1, :] + b_ref[...]
    for tap in range(hist):
        y = y + state_ref[tap] * w_ref[tap:tap + 1, :]
    y = _layer_norm(y, g_ref[...], bn_ref[...])
    o_ref[...] = (y * _sigmoid(y)).astype(o_ref.dtype)


def _conv_sample(state_t, u, conv_w, conv_b, g, bn):
    full = lambda a: pl.BlockSpec(a.shape, lambda i: (0,) * a.ndim)
    args = (state_t, u, conv_w, conv_b, g, bn)
    return pl.pallas_call(_conv_sample_kernel, grid=(1,), in_specs=[full(a) for a in args],
                          out_specs=full(u), out_shape=jax.ShapeDtypeStruct(u.shape, BF16),
                          compiler_params=_params("arbitrary"), name="conv_sample")(*args)


def _merge_router_kernel(att_ref, conv_ref, ga_ref, gc_ref, x_ref, wa_ref, wc_ref, wo_ref,
                         g1_ref, b1_ref, wr_ref, br_ref, cnt_in_ref,
                         h_ref, idx_ref, wts_ref, rank_ref, cnt_ref):
    tm = x_ref.shape[0]

    @pl.when(pl.program_id(0) == 0)
    def _():
        cnt_ref[...] = cnt_in_ref[...]

    a = _dot(att_ref[...], wa_ref[...])
    c = _dot(conv_ref[...], wc_ref[...])
    mixed = (ga_ref[...] * a + gc_ref[...] * c).astype(BF16)
    res = DEEPNORM_ALPHA * x_ref[...] + _dot(mixed, wo_ref[...])
    h = _layer_norm(res, g1_ref[...], b1_ref[...])
    h_ref[...] = h

    hh, hm, hl = _split3(h)
    w_hi, w_mid, w_lo = wr_ref[0], wr_ref[1], wr_ref[2]
    logits = (_dot(hh, w_hi) + _dot(hh, w_mid) + _dot(hm, w_hi)
              + _dot(hh, w_lo) + _dot(hm, w_mid) + _dot(hl, w_hi)) + br_ref[...]

    eid = lax.broadcasted_iota(jnp.int32, (tm, N_EXPERTS), 1).astype(F32)
    k_lane = lax.broadcasted_iota(jnp.int32, (tm, TOP_K), 1)
    remaining = logits
    chosen = jnp.zeros((tm, N_EXPERTS), F32)
    vals, picks = [], []
    for _ in range(TOP_K):
        mx = jnp.max(remaining, axis=-1, keepdims=True)
        pick = jnp.min(jnp.where(remaining == mx, eid, N_EXPERTS), axis=-1, keepdims=True)
        hit = eid == pick
        chosen = jnp.where(hit, 1.0, chosen)
        remaining = jnp.where(hit, -jnp.inf, remaining)
        vals.append(mx)
        picks.append(pick)
    exps = [jnp.exp(v - vals[0]) for v in vals]
    denom = exps[0] + exps[1] + exps[2] + exps[3]

    rr = lax.broadcasted_iota(jnp.int32, (tm, tm), 0)
    cc = lax.broadcasted_iota(jnp.int32, (tm, tm), 1)
    strict_lower = jnp.where(cc < rr, 1.0, 0.0).astype(BF16)
    rank_dense = _dot(strict_lower, chosen.astype(BF16)) + cnt_ref[...]

    idx_out = jnp.zeros((tm, TOP_K), F32)
    wts_out = jnp.zeros((tm, TOP_K), F32)
    rank_out = jnp.zeros((tm, TOP_K), F32)
    for k in range(TOP_K):
        rk = jnp.sum(jnp.where(eid == picks[k], rank_dense, 0.0), axis=-1, keepdims=True)
        idx_out = jnp.where(k_lane == k, picks[k], idx_out)
        wts_out = jnp.where(k_lane == k, exps[k] / denom, wts_out)
        rank_out = jnp.where(k_lane == k, rk, rank_out)
    idx_ref[...] = idx_out.astype(jnp.int32)
    wts_ref[...] = wts_out
    rank_ref[...] = rank_out.astype(jnp.int32)
    cnt_ref[...] = cnt_ref[...] + jnp.sum(chosen, axis=0, keepdims=True)


def _merge_router(att, conv, ga, gc, x, wa, wc, wo, g1, b1, wr3, br, cnt_in, tm):
    n = x.shape[0]
    row = lambda i: (i, 0)
    tile = pl.BlockSpec((tm, D_MODEL), row)
    small = pl.BlockSpec((tm, TOP_K), row)
    const = lambda a: pl.BlockSpec(a.shape, lambda i: (0,) * a.ndim)
    out_shape = (jax.ShapeDtypeStruct((n, D_MODEL), F32),
                 jax.ShapeDtypeStruct((n, TOP_K), jnp.int32),
                 jax.ShapeDtypeStruct((n, TOP_K), F32),
                 jax.ShapeDtypeStruct((n, TOP_K), jnp.int32),
                 jax.ShapeDtypeStruct((1, N_EXPERTS), F32))
    return pl.pallas_call(
        _merge_router_kernel, grid=(n // tm,),
        in_specs=[tile, tile, tile, tile, tile, const(wa), const(wc), const(wo),
                  const(g1), const(b1), const(wr3), const(br), const(cnt_in)],
        out_specs=(tile, small, small, small, const(cnt_in)), out_shape=out_shape,
        compiler_params=_params("arbitrary"), name="merge_router",
    )(att, conv, ga, gc, x, wa, wc, wo, g1, b1, wr3, br, cnt_in)


def _row_copy(src_ref, src_row, dst_ref, dst_row, sem):
    return pltpu.make_async_copy(src_ref.at[pl.ds(src_row, 1), :], dst_ref.at[pl.ds(dst_row, 1), :], sem)


def _dispatch_kernel(pos_ref, h_ref, xs_in_ref, xs_ref, sem):
    del xs_in_ref
    tm = h_ref.shape[0]

    def issue(t, carry):
        for k in range(TOP_K):
            _row_copy(h_ref, t, xs_ref, pos_ref[0, 0, t * TOP_K + k], sem).start()
        return carry

    lax.fori_loop(0, tm, issue, 0)
    for _ in range(TOP_K):
        pltpu.make_async_copy(h_ref, xs_ref.at[pl.ds(0, tm), :], sem).wait()


def _dispatch(pos, h, xs, tm):
    n = h.shape[0]
    pos3 = pos.reshape(n // tm, 1, tm * TOP_K)
    return pl.pallas_call(
        _dispatch_kernel, grid=(n // tm,),
        in_specs=[pl.BlockSpec((1, 1, tm * TOP_K), lambda i: (i, 0, 0), memory_space=pltpu.SMEM),
                  pl.BlockSpec((tm, D_MODEL), lambda i: (i, 0)),
                  pl.BlockSpec(memory_space=pl.ANY)],
        out_specs=pl.BlockSpec(memory_space=pl.ANY),
        out_shape=jax.ShapeDtypeStruct(xs.shape, xs.dtype),
        scratch_shapes=[pltpu.SemaphoreType.DMA(())],
        input_output_aliases={2: 0},
        compiler_params=_params("arbitrary"), name="moe_dispatch")(pos3, h, xs)


def _expert_weight_copies(wgu_hbm, wd_hbm, wgu_buf, wd_buf, sems, expert, slot):
    return (pltpu.make_async_copy(wgu_hbm.at[expert], wgu_buf.at[slot], sems.at[0, slot]),
            pltpu.make_async_copy(wd_hbm.at[expert], wd_buf.at[slot], sems.at[1, slot]))


def _experts_kernel(te_ref, tv_ref, slot_ref, nxt_ref, xs_ref, wgu_hbm, bgu_ref, wd_hbm, bd_ref, o_ref,
                    wgu_buf, wd_buf, wgu_bf, wd_bf, sems):
    n = pl.program_id(0)
    prev = te_ref[jnp.maximum(n - 1, 0)]
    copies = functools.partial(_expert_weight_copies, wgu_hbm, wd_hbm, wgu_buf, wd_buf, sems)

    @pl.when(n == 0)
    def _():
        for c in copies(te_ref[0], slot_ref[0]):
            c.start()

    @pl.when((n == 0) | (te_ref[n] != prev))
    def _():
        slot = slot_ref[n]
        for c in copies(te_ref[n], slot):
            c.wait()

        @pl.when(nxt_ref[n] >= 0)
        def _():
            for c in copies(nxt_ref[n], 1 - slot):
                c.start()

        wgu_bf[...] = wgu_buf[slot].astype(BF16)
        wd_bf[...] = wd_buf[slot].astype(BF16)

    @pl.when(tv_ref[n] == 1)
    def _():
        gu = _dot(xs_ref[...].astype(BF16), wgu_bf[...]) + bgu_ref[0]
        gate = jnp.minimum(gu[:, :D_FF], SWIGLU_LIMIT)
        up = jnp.clip(gu[:, D_FF:], -SWIGLU_LIMIT, SWIGLU_LIMIT)
        act = (up + 1.0) * gate * _sigmoid(SWIGLU_ALPHA * gate)
        o_ref[...] = _dot(act.astype(BF16), wd_bf[...]) + bd_ref[0]

    @pl.when(tv_ref[n] == 0)
    def _():
        o_ref[...] = jnp.zeros(o_ref.shape, F32)


def _experts(tile_expert, tile_valid, tile_slot, next_expert, xs, w_gate_up, b_gate_up, w_down, b_down):
    rows = xs.shape[0]
    tm = ROW_TILE
    e3 = lambda n, te, *_: (te[n], 0, 0)
    row = lambda n, *_: (n, 0)
    grid_spec = pltpu.PrefetchScalarGridSpec(
        num_scalar_prefetch=4, grid=(rows // tm,),
        in_specs=[pl.BlockSpec((tm, D_MODEL), row),
                  pl.BlockSpec(memory_space=pl.ANY),
                  pl.BlockSpec((1, 1, 2 * D_FF), e3),
                  pl.BlockSpec(memory_space=pl.ANY),
                  pl.BlockSpec((1, 1, D_MODEL), e3)],
        out_specs=pl.BlockSpec((tm, D_MODEL), row),
        scratch_shapes=[pltpu.VMEM((2, D_MODEL, 2 * D_FF), F32), pltpu.VMEM((2, D_FF, D_MODEL), F32),
                        pltpu.VMEM((D_MODEL, 2 * D_FF), BF16), pltpu.VMEM((D_FF, D_MODEL), BF16),
                        pltpu.SemaphoreType.DMA((2, 2))])
    return pl.pallas_call(_experts_kernel, grid_spec=grid_spec,
                          out_shape=jax.ShapeDtypeStruct((rows, D_MODEL), F32),
                          compiler_params=_params("arbitrary"), name="moe_experts",
                          )(tile_expert, tile_valid, tile_slot, next_expert, xs, w_gate_up,
                            b_gate_up.reshape(N_EXPERTS, 1, 2 * D_FF), w_down,
                            b_down.reshape(N_EXPERTS, 1, D_MODEL))


def _combine_kernel(pos_ref, pos_next_ref, wts_ref, h_ref, ys_ref, g_ref, b_ref, o_ref, buf_ref, sems):
    tm = h_ref.shape[0]
    i = pl.program_id(0)
    slot = i % 2

    def gather(positions, dst_slot):
        def issue(t, carry):
            for k in range(TOP_K):
                _row_copy(ys_ref, positions[0, 0, t * TOP_K + k], buf_ref.at[dst_slot, k], t,
                          sems.at[dst_slot]).start()
            return carry

        lax.fori_loop(0, tm, issue, 0)

    @pl.when(i == 0)
    def _():
        gather(pos_ref, slot)

    @pl.when(i + 1 < pl.num_programs(0))
    def _():
        gather(pos_next_ref, 1 - slot)

    for k in range(TOP_K):
        pltpu.make_async_copy(ys_ref.at[pl.ds(0, tm), :], buf_ref.at[slot, k], sems.at[slot]).wait()
    wts = wts_ref[...]
    moe = wts[:, 0:1] * buf_ref[slot, 0]
    for k in range(1, TOP_K):
        moe = moe + wts[:, k:k + 1] * buf_ref[slot, k]
    o_ref[...] = _layer_norm(DEEPNORM_ALPHA * h_ref[...] + moe, g_ref[...], b_ref[...])


def _combine(pos, wts, h, ys, g2, b2, tm):
    n = h.shape[0]
    pos3 = pos.reshape(n // tm, 1, tm * TOP_K)
    steps = n // tm
    row = lambda i: (i, 0)
    const = lambda a: pl.BlockSpec(a.shape, lambda i: (0,) * a.ndim)
    pos_block = lambda index: pl.BlockSpec((1, 1, tm * TOP_K), index, memory_space=pltpu.SMEM)
    return pl.pallas_call(
        _combine_kernel, grid=(steps,),
        in_specs=[pos_block(lambda i: (i, 0, 0)),
                  pos_block(lambda i: (jnp.minimum(i + 1, steps - 1), 0, 0)),
                  pl.BlockSpec((tm, TOP_K), row),
                  pl.BlockSpec((tm, D_MODEL), row),
                  pl.BlockSpec(memory_space=pl.ANY), const(g2), const(b2)],
        out_specs=pl.BlockSpec((tm, D_MODEL), row),
        out_shape=jax.ShapeDtypeStruct((n, D_MODEL), F32),
        scratch_shapes=[pltpu.VMEM((2, TOP_K, tm, D_MODEL), F32), pltpu.SemaphoreType.DMA((2,))],
        compiler_params=_params("arbitrary"), name="moe_combine")(pos3, pos3, wts, h, ys, g2, b2)


def _split_in_proj(w_in, b_forget):
    a = N_HEADS * HEAD_DIM
    cuts = [0, a, 2 * a, 3 * a]
    f0 = 3 * a
    rest = f0 + N_HEADS
    starts = cuts[:3] + [rest + i * D_MODEL for i in range(4)]
    w_main = jnp.stack([w_in[:, s:s + D_MODEL] for s in starts]).astype(BF16)
    w_t = jnp.stack([w_in[:, s:s + a].T for s in cuts[1:3]]).astype(BF16)
    w_f = w_in[:, f0:rest].astype(BF16)
    return w_main, w_t, w_f, w_f.T, b_forget.reshape(1, N_HEADS), b_forget.reshape(N_HEADS, 1)


def _routing_tables(counts, n_tiles):
    cnt = counts.reshape(N_EXPERTS).astype(jnp.int32)
    tiles = (cnt + ROW_TILE - 1) // ROW_TILE
    tile_end = jnp.cumsum(tiles)
    start_row = (tile_end - tiles) * ROW_TILE
    n = jnp.arange(n_tiles, dtype=jnp.int32)
    valid = n < tile_end[-1]
    owner = jnp.sum((n[:, None] >= tile_end[None, :]).astype(jnp.int32), axis=1)
    last_owner = jnp.sum((tile_end[-1] - 1 >= tile_end).astype(jnp.int32))
    tile_expert = jnp.where(valid, owner, last_owner).astype(jnp.int32)
    experts = jnp.arange(N_EXPERTS, dtype=jnp.int32)[None, :]
    later = (tiles[None, :] > 0) & (experts > tile_expert[:, None])
    earlier = (tiles[None, :] > 0) & (experts < tile_expert[:, None])
    next_expert = jnp.min(jnp.where(later, experts, N_EXPERTS), axis=1)
    next_expert = jnp.where(next_expert == N_EXPERTS, -1, next_expert).astype(jnp.int32)
    tile_slot = (jnp.sum(earlier.astype(jnp.int32), axis=1) % 2).astype(jnp.int32)
    return start_row, tile_expert, valid.astype(jnp.int32), tile_slot, next_expert


def _layer(xp, xs, cache_k, cache_v, cache_logf, state_conv, page_table,
           w_in, b_forget, conv_w, conv_b, conv_norm_g, conv_norm_b,
           w_attn_proj, w_conv_proj, w_out, ln1_g, ln1_b,
           w_router, b_router, w_gate_up, b_gate_up, w_down, b_down, ln2_g, ln2_b):
    batch, seq, d = xp.shape
    n_req = xs.shape[0]
    n_p = batch * seq
    row = lambda a: a.reshape(1, -1)
    heads = lambda a: a.reshape(a.shape[0], N_HEADS, HEAD_DIM)

    w_main, w_t, w_f, w_ft, b_f, b_ft = _split_in_proj(w_in, b_forget)
    wa, wc, wo = (w.astype(BF16) for w in (w_attn_proj, w_conv_proj, w_out))
    wr3 = jnp.stack(_split3(w_router))
    cw, cb, cg, cbn = conv_w, row(conv_b), row(conv_norm_g), row(conv_norm_b)
    g1, b1, g2, b2, br = row(ln1_g), row(ln1_b), row(ln2_g), row(ln2_b), row(b_router)

    xp2 = xp.reshape(n_p, d)
    qb, u_p, ga_p, gc_p, kt_p, vt_p, lft_p, ke, ve = _in_projection(
        xp2, w_main, w_f, w_ft, b_f, b_ft, TOKEN_TILE, w_t, seq)
    att_p = _prompt_attention(qb, ke, ve, batch, seq)
    conv_p = _conv_prompt(u_p, cw, cb, cg, cbn, batch, seq)

    xs2 = xs.reshape(n_req, d)
    q_s, u_s, ga_s, gc_s, k_s, v_s, lf_s, lft_s = _in_projection(
        xs2, w_main, w_f, w_ft, b_f, b_ft, n_req)
    att_s = _sample_attention(heads(q_s), heads(k_s), heads(v_s), lft_s,
                              jnp.transpose(cache_k, (0, 2, 3, 1)), jnp.transpose(cache_v, (0, 2, 3, 1)),
                              jnp.transpose(cache_logf, (0, 2, 1)), page_table)
    state_t = jnp.transpose(state_conv, (1, 0, 2))
    conv_s = _conv_sample(state_t, u_s, cw, cb, cg, cbn)

    zero_cnt = jnp.zeros((1, N_EXPERTS), F32)
    h_p, idx_p, wts_p, rank_p, cnt_p = _merge_router(
        att_p, conv_p, ga_p, gc_p, xp2, wa, wc, wo, g1, b1, wr3, br, zero_cnt, TOKEN_TILE)
    h_s, idx_s, wts_s, rank_s, cnt = _merge_router(
        att_s, conv_s, ga_s, gc_s, xs2, wa, wc, wo, g1, b1, wr3, br, cnt_p, n_req)

    n_tok = n_p + n_req
    n_tiles = (n_tok * TOP_K + N_EXPERTS * (ROW_TILE - 1) + ROW_TILE - 1) // ROW_TILE
    start_row, tile_expert, tile_valid, tile_slot, next_expert = _routing_tables(cnt, n_tiles)
    pos_p = start_row[idx_p] + rank_p
    pos_s = start_row[idx_s] + rank_s
    sorted_rows = jnp.zeros((n_tiles * ROW_TILE, d), F32)
    sorted_rows = _dispatch(pos_p, h_p, sorted_rows, TOKEN_TILE)
    sorted_rows = _dispatch(pos_s, h_s, sorted_rows, n_req)
    expert_out = _experts(tile_expert, tile_valid, tile_slot, next_expert, sorted_rows,
                          w_gate_up, b_gate_up, w_down, b_down)
    y_p = _combine(pos_p, wts_p, h_p, expert_out, g2, b2, TOKEN_TILE)
    y_s = _combine(pos_s, wts_s, h_s, expert_out, g2, b2, n_req)

    hist = CONV_WIDTH - 1
    token_major = lambda t: jnp.transpose(t.reshape(batch, N_HEADS, HEAD_DIM, seq), (0, 3, 1, 2))
    conv_state_p = u_p.reshape(batch, seq, d)[:, seq - hist:, :]
    conv_state_s = jnp.transpose(jnp.concatenate([state_t[1:], u_s[None]], axis=0), (1, 0, 2))
    return (y_p.reshape(batch, seq, d), y_s.reshape(n_req, 1, d),
            token_major(kt_p), token_major(vt_p), jnp.transpose(lft_p, (0, 2, 1)), conv_state_p,
            k_s.reshape(n_req, 1, N_HEADS, HEAD_DIM), v_s.reshape(n_req, 1, N_HEADS, HEAD_DIM),
            lf_s.reshape(n_req, 1, N_HEADS), conv_state_s)


def kernel(x_prompt, x_sample, cache_k, cache_v, cache_logf, state_conv, page_table, w_in, b_forget, conv_w, conv_b, conv_norm_g, conv_norm_b, w_attn_proj, w_conv_proj, w_out, ln1_g, ln1_b, w_router, b_router, w_gate_up, b_gate_up, w_down, b_down, ln2_g, ln2_b):
    assert x_prompt.shape[-1] == D_MODEL and w_in.shape[0] == DEPTH
    out = _layer(x_prompt, x_sample, cache_k[0], cache_v[0], cache_logf[0], state_conv[0], page_table,
                 w_in[0], b_forget[0], conv_w[0], conv_b[0], conv_norm_g[0], conv_norm_b[0],
                 w_attn_proj[0], w_conv_proj[0], w_out[0], ln1_g[0], ln1_b[0],
                 w_router[0], b_router[0], w_gate_up[0], b_gate_up[0], w_down[0], b_down[0],
                 ln2_g[0], ln2_b[0])
    y_p, y_s = out[0], out[1]
    return (y_p, y_s) + tuple(o[None] for o in out[2:])
```

```python
import functools
import math

import jax
import jax.numpy as jnp
from jax import lax
from jax.experimental import pallas as pl
from jax.experimental.pallas import tpu as pltpu

F32 = jnp.float32
BF16 = jnp.bfloat16

D_MODEL = 1024
N_HEADS = 16
HEAD_DIM = 64
CONV_WIDTH = 31
N_EXPERTS = 32
TOP_K = 4
D_FF = 1024
PAGE_SIZE = 128
SWIGLU_ALPHA = 1.702
SWIGLU_LIMIT = 7.0
LN_EPS = 1e-5
DEPTH = 1
DEEPNORM_ALPHA = (2 * DEPTH) ** 0.25
LOG2E = math.log2(math.e)

LANES = 128
SUBLANES = 8
HEADS_PER_LANE_TILE = LANES // HEAD_DIM
VMEM_LIMIT = 56 * 2 ** 20

TOKEN_TILE = 256
ATTN_TILE = 512
ROW_TILE = 256
PAGES_PER_STEP = 8
N_BIAS = 3
KEY_EXT = N_HEADS * LANES


def _params(*sem):
    return pltpu.CompilerParams(dimension_semantics=sem, vmem_limit_bytes=VMEM_LIMIT)


def _split3(x):
    hi = x.astype(BF16)
    r = x - hi.astype(F32)
    mid = r.astype(BF16)
    lo = (r - mid.astype(F32)).astype(BF16)
    return hi, mid, lo


def _dot(a, b):
    return jnp.dot(a, b, preferred_element_type=F32)


def _cumsum_lanes(x, upper, carry):
    hi, mid, lo = _split3(x)
    return _dot(hi, upper) + _dot(mid, upper) + _dot(lo, upper) + carry


def _upper_ones(n):
    rr = lax.broadcasted_iota(jnp.int32, (n, n), 0)
    cc = lax.broadcasted_iota(jnp.int32, (n, n), 1)
    return jnp.where(rr <= cc, 1.0, 0.0).astype(BF16)


def _log_sigmoid(x):
    return jnp.minimum(x, 0.0) - jnp.log1p(jnp.exp(-jnp.abs(x)))


def _sigmoid(x):
    return 1.0 / (1.0 + jnp.exp(-x))


def _layer_norm(x, g, b):
    mu = jnp.mean(x, axis=-1, keepdims=True)
    xc = x - mu
    var = jnp.mean(xc * xc, axis=-1, keepdims=True)
    return xc * lax.rsqrt(var + LN_EPS) * g + b


def _tree(xs, op):
    xs = list(xs)
    while len(xs) > 1:
        xs = [op(xs[i], xs[i + 1]) if i + 1 < len(xs) else xs[i] for i in range(0, len(xs), 2)]
    return xs[0]


def _extended_rows(rows, extra, head):
    pad = jnp.zeros((HEAD_DIM - SUBLANES, rows.shape[1]), F32)
    return [rows, extra, pad] if head % 2 == 0 else [extra, pad, rows]


def _inproj_kernel(*refs, emit_keys, tiles_per_seq):
    if emit_keys:
        (x_ref, w_ref, wf_ref, wft_ref, bf_ref, bft_ref, wt_ref,
         q_ref, u_ref, ga_ref, gc_ref, kt_ref, vt_ref, lft_ref, ke_ref, ve_ref, carry_ref) = refs
    else:
        (x_ref, w_ref, wf_ref, wft_ref, bf_ref, bft_ref,
         q_ref, u_ref, ga_ref, gc_ref, k_ref, v_ref, lf_ref, lft_ref) = refs
    tm = x_ref.shape[0]
    nt = (((1,), (1,)), ((), ()))
    x = x_ref[...].astype(BF16)
    q = _dot(x, w_ref[0])
    q_ref[...] = (q * (LOG2E * HEAD_DIM ** -0.5)).astype(BF16)
    u_ref[...] = _dot(x, w_ref[3]) * _sigmoid(_dot(x, w_ref[4]))
    ga_ref[...] = _sigmoid(_dot(x, w_ref[5]))
    gc_ref[...] = _sigmoid(_dot(x, w_ref[6]))
    ft = lax.dot_general(wft_ref[...], x, nt, preferred_element_type=F32)
    lft = _log_sigmoid(ft + bft_ref[...])
    if not emit_keys:
        k_ref[...] = _dot(x, w_ref[1])
        v_ref[...] = _dot(x, w_ref[2])
        lf_ref[...] = _log_sigmoid(_dot(x, wf_ref[...]) + bf_ref[...])
        lft_ref[...] = lft
        return

    kt = lax.dot_general(wt_ref[0], x, nt, preferred_element_type=F32)
    vt = lax.dot_general(wt_ref[1], x, nt, preferred_element_type=F32)
    kt_ref[0] = kt
    vt_ref[0] = vt
    lft_ref[0] = lft

    @pl.when(pl.program_id(0) % tiles_per_seq == 0)
    def _():
        carry_ref[...] = jnp.zeros(carry_ref.shape, F32)

    csum = _cumsum_lanes(lft * LOG2E, _upper_ones(tm), carry_ref[...])
    carry_ref[...] = csum[:, tm - 1:tm]
    pieces = [p.astype(F32) for p in _split3(-csum)]
    sub = lax.broadcasted_iota(jnp.int32, (SUBLANES, tm), 0)
    ones_rows = jnp.where(sub == 0, 1.0, 0.0)
    k_blocks, v_blocks = [], []
    for h in range(N_HEADS):
        bias_rows = jnp.zeros((SUBLANES, tm), F32)
        for i, piece in enumerate(pieces):
            bias_rows = jnp.where(sub == i, piece[h:h + 1, :], bias_rows)
        feat = slice(h * HEAD_DIM, (h + 1) * HEAD_DIM)
        k_blocks += _extended_rows(kt[feat], bias_rows, h)
        v_blocks += _extended_rows(vt[feat], ones_rows, h)
    ke_ref[0] = jnp.concatenate(k_blocks, axis=0).astype(BF16)
    ve_ref[0] = jnp.concatenate(v_blocks, axis=0).astype(BF16)


def _in_projection(x, w_main, w_f, w_ft, b_f, b_ft, tm, w_t=None, seq=None):
    emit_keys = seq is not None
    n = x.shape[0]
    row = lambda i: (i, 0)
    const = lambda a: pl.BlockSpec(a.shape, lambda i: (0,) * a.ndim)
    once = lambda a: pl.BlockSpec(a.shape, lambda i: (0,) * a.ndim, pipeline_mode=pl.Buffered(1))
    big = lambda dt: jax.ShapeDtypeStruct((n, D_MODEL), dt)
    tile = pl.BlockSpec((tm, D_MODEL), row)
    args = [x, w_main, w_f, w_ft, b_f, b_ft]
    in_specs = [tile, once(w_main), const(w_f), const(w_ft), const(b_f), const(b_ft)]
    out_shape = [big(BF16), big(F32), big(F32), big(F32)]
    out_specs = [tile, tile, tile, tile]
    scratch = []
    if emit_keys:
        tiles_per_seq = seq // tm
        batch = n // seq
        seq_major = lambda rows, dt: jax.ShapeDtypeStruct((batch, rows, seq), dt)
        seq_tile = lambda rows: pl.BlockSpec(
            (1, rows, tm), lambda i: (i // tiles_per_seq, 0, i % tiles_per_seq))
        args.append(w_t)
        in_specs.append(once(w_t))
        out_shape += [seq_major(D_MODEL, F32), seq_major(D_MODEL, F32), seq_major(N_HEADS, F32),
                      seq_major(KEY_EXT, BF16), seq_major(KEY_EXT, BF16)]
        out_specs += [seq_tile(D_MODEL), seq_tile(D_MODEL), seq_tile(N_HEADS),
                      seq_tile(KEY_EXT), seq_tile(KEY_EXT)]
        scratch = [pltpu.VMEM((N_HEADS, 1), F32)]
    else:
        tiles_per_seq = 1
        out_shape += [big(F32), big(F32), jax.ShapeDtypeStruct((n, N_HEADS), F32),
                      jax.ShapeDtypeStruct((N_HEADS, n), F32)]
        out_specs += [tile, tile, pl.BlockSpec((tm, N_HEADS), row),
                      pl.BlockSpec((N_HEADS, tm), lambda i: (0, i))]
    body = functools.partial(_inproj_kernel, emit_keys=emit_keys, tiles_per_seq=tiles_per_seq)
    return pl.pallas_call(body, grid=(n // tm,), in_specs=in_specs, out_specs=out_specs,
                          out_shape=out_shape, scratch_shapes=scratch,
                          compiler_params=_params("arbitrary"), name="in_projection")(*args)


def _flash_kernel(q_ref, ke_ref, ve_ref, o_ref):
    t = ATTN_TILE
    i = pl.program_id(2)
    lane = lax.broadcasted_iota(jnp.int32, (1, LANES), 1)
    first = lane < HEAD_DIM
    ones_at = lambda cond: jnp.where(cond, 1.0, 0.0).astype(BF16)
    q2 = q_ref[...]
    q_ext = (jnp.where(first, q2, ones_at(lane < HEAD_DIM + N_BIAS)),
             jnp.where(first, ones_at(lane < N_BIAS), q2))
    den_lane = (HEAD_DIM, 0)
    nt = (((1,), (1,)), ((), ()))

    def step(j, carry, masked):
        ks = pl.multiple_of(j * t, t)
        out = []
        for h in range(HEADS_PER_LANE_TILE):
            m, acc = carry[2 * h], carry[2 * h + 1]
            rows = slice(h * LANES, (h + 1) * LANES)
            s = _dot(q_ext[h], ke_ref[0, rows, pl.ds(ks, t)])
            if masked:
                rr = lax.broadcasted_iota(jnp.int32, (t, t), 0)
                cc = lax.broadcasted_iota(jnp.int32, (t, t), 1)
                s = jnp.where(cc <= rr, s, -jnp.inf)
            m_new = jnp.maximum(m, jnp.max(s, axis=-1, keepdims=True))
            alpha = jnp.exp2(m - m_new)
            p = jnp.exp2(s - m_new).astype(BF16)
            pv = lax.dot_general(p, ve_ref[0, rows, pl.ds(ks, t)], nt, preferred_element_type=F32)
            out += [m_new, alpha * acc + pv]
        return tuple(out)

    neg = jnp.full((t, 1), -jnp.inf, F32)
    za = jnp.zeros((t, LANES), F32)
    carry = lax.fori_loop(0, i, lambda j, c: step(j, c, False), (neg, za, neg, za))
    _, acc0, _, acc1 = step(i, carry, True)
    inv0 = 1.0 / acc0[:, den_lane[0]:den_lane[0] + 1]
    inv1 = 1.0 / acc1[:, den_lane[1]:den_lane[1] + 1]
    o_ref[...] = jnp.where(first, acc0 * inv0, acc1 * inv1).astype(o_ref.dtype)


def _prompt_attention(qb, ke, ve, batch, seq):
    t = ATTN_TILE
    nq = seq // t
    pairs = N_HEADS // HEADS_PER_LANE_TILE
    q_spec = pl.BlockSpec((t, LANES), lambda b, hp, i: (b * nq + i, hp))
    e_spec = pl.BlockSpec((1, HEADS_PER_LANE_TILE * LANES, seq), lambda b, hp, i: (b, hp, 0))
    return pl.pallas_call(_flash_kernel, grid=(batch, pairs, nq),
                          in_specs=[q_spec, e_spec, e_spec], out_specs=q_spec,
                          out_shape=jax.ShapeDtypeStruct(qb.shape, BF16),
                          compiler_params=_params("parallel", "parallel", "arbitrary"),
                          name="prompt_attention")(qb, ke, ve)


def _paged_kernel(pt_ref, *refs):
    pp = PAGES_PER_STEP
    k_refs, v_refs, lf_refs = refs[:pp], refs[pp:2 * pp], refs[2 * pp:3 * pp]
    qrep_ref, q_ref, kn_ref, vn_ref, lfnt_ref, o_ref, m_ref, l_ref, c_ref, acc_ref = refs[3 * pp:]
    r = pl.program_id(0)
    g = pl.program_id(1)
    wide = (N_HEADS, PAGE_SIZE)

    @pl.when(g == 0)
    def _():
        m_ref[...] = jnp.full(m_ref.shape, -jnp.inf, F32)
        l_ref[...] = jnp.zeros(l_ref.shape, F32)
        c_ref[...] = jnp.zeros(c_ref.shape, F32)
        acc_ref[...] = jnp.zeros(acc_ref.shape, F32)

    upper = _upper_ones(PAGE_SIZE)
    carry = c_ref[...]
    logits = []
    for p_i in range(pp):
        incl = _cumsum_lanes(lf_refs[p_i][0] * LOG2E, upper, carry)
        carry = jnp.broadcast_to(incl[:, PAGE_SIZE - 1:PAGE_SIZE], wide)
        qk = [jnp.sum(k_refs[p_i][0, h] * qrep_ref[0, h], axis=0, keepdims=True) for h in range(N_HEADS)]
        logits.append(jnp.concatenate(qk, axis=0) - incl)
    c_ref[...] = carry
    m_old = m_ref[...]
    step_max = jnp.max(_tree(logits, jnp.maximum), axis=-1, keepdims=True)
    m_new = jnp.maximum(m_old, jnp.broadcast_to(step_max, wide))
    m_ref[...] = m_new
    alpha = jnp.exp2(m_old - m_new)
    probs = [jnp.exp2(lg - m_new) for lg in logits]
    l_ref[...] = alpha * l_ref[...] + _tree(probs, jnp.add)
    for h in range(N_HEADS):
        head = slice(h, h + 1)
        pv = _tree([probs[p_i][head, :] * v_refs[p_i][0, h] for p_i in range(pp)], jnp.add)
        acc_ref[h] = acc_ref[h] * alpha[head, :] + pv

    @pl.when(g == pl.num_programs(1) - 1)
    def _():
        q = q_ref[0].astype(F32)
        s_new = jnp.sum(kn_ref[0] * q, axis=-1, keepdims=True)
        lane = lax.broadcasted_iota(jnp.int32, lfnt_ref.shape, 1)
        lf_new = jnp.sum(jnp.where(lane == r, lfnt_ref[...], 0.0), axis=-1, keepdims=True)
        logit = s_new - (c_ref[:, 0:1] + lf_new * LOG2E)
        m_last = m_ref[:, 0:1]
        m_fin = jnp.maximum(m_last, logit)
        a_fin = jnp.exp2(m_last - m_fin)
        p_new = jnp.exp2(logit - m_fin)
        l_fin = a_fin * jnp.sum(l_ref[...], axis=-1, keepdims=True) + p_new
        for h in range(N_HEADS):
            head = slice(h, h + 1)
            tot = jnp.sum(acc_ref[h], axis=-1, keepdims=True)
            o_ref[0, h] = (a_fin[head] * tot + p_new[head] * vn_ref[0, h]) / l_fin[head]


def _sample_attention(q_s, k_s, v_s, lft_s, cache_kt, cache_vt, cache_lft, page_table):
    n_req, n_pages = page_table.shape
    pp = PAGES_PER_STEP
    steps = n_pages // pp
    q_rep = jnp.broadcast_to(q_s.astype(F32)[..., None], q_s.shape + (PAGE_SIZE,))

    def page_map(p_i, nd):
        return lambda r, g, pt: (pt[r * n_pages + g * pp + p_i],) + (0,) * nd

    k_specs = [pl.BlockSpec((1, N_HEADS, HEAD_DIM, PAGE_SIZE), page_map(p, 3)) for p in range(pp)]
    lf_specs = [pl.BlockSpec((1, N_HEADS, PAGE_SIZE), page_map(p, 2)) for p in range(pp)]
    req3 = lambda r, g, pt: (r, 0, 0)
    req4 = lambda r, g, pt: (r, 0, 0, 0)
    head = pl.BlockSpec((1, N_HEADS, HEAD_DIM), req3)
    column = pl.BlockSpec((1, N_HEADS, HEAD_DIM, 1), req4)
    wide = lambda: pltpu.VMEM((N_HEADS, PAGE_SIZE), F32)
    grid_spec = pltpu.PrefetchScalarGridSpec(
        num_scalar_prefetch=1, grid=(n_req, steps),
        in_specs=k_specs + k_specs + lf_specs
        + [pl.BlockSpec((1, N_HEADS, HEAD_DIM, PAGE_SIZE), req4), head, head, column,
           pl.BlockSpec(lft_s.shape, lambda r, g, pt: (0, 0))],
        out_specs=column,
        scratch_shapes=[wide(), wide(), wide(), pltpu.VMEM((N_HEADS, HEAD_DIM, PAGE_SIZE), F32)])
    args = [cache_kt] * pp + [cache_vt] * pp + [cache_lft] * pp
    args += [q_rep, q_s, k_s, v_s[..., None], lft_s]
    out = pl.pallas_call(_paged_kernel, grid_spec=grid_spec,
                         out_shape=jax.ShapeDtypeStruct((n_req, N_HEADS, HEAD_DIM, 1), F32),
                         compiler_params=_params("arbitrary", "arbitrary"),
                         name="sample_attention")(page_table.reshape(-1), *args)
    return out.reshape(n_req, N_HEADS * HEAD_DIM).astype(BF16)


def _conv_prompt_kernel(u_ref, w_ref, b_ref, g_ref, bn_ref, o_ref, buf_ref, y_ref):
    ts = u_ref.shape[0]
    halo = 32
    j = pl.program_id(1)

    @pl.when(j == 0)
    def _():
        buf_ref[0:halo, :] = jnp.zeros((halo, D_MODEL), F32)

    buf_ref[halo:halo + ts, :] = u_ref[...]
    first_off = halo - (CONV_WIDTH - 1)
    for c in range(D_MODEL // LANES):
        sl = slice(c * LANES, (c + 1) * LANES)
        y = None
        for r in range(SUBLANES):
            rows = ts if r == 0 else ts + SUBLANES
            z = None
            for a in range((halo + SUBLANES) // SUBLANES):
                tap = a * SUBLANES + r - first_off
                if 0 <= tap < CONV_WIDTH:
                    term = buf_ref[a * SUBLANES:a * SUBLANES + rows, sl] * w_ref[tap:tap + 1, sl]
                    z = term if z is None else z + term
            part = z if r == 0 else z[r:r + ts]
            y = part if y is None else y + part
        y_ref[:, sl] = y + b_ref[:, sl]
    buf_ref[0:halo, :] = buf_ref[ts:ts + halo, :]
    y = _layer_norm(y_ref[...], g_ref[...], bn_ref[...])
    o_ref[...] = (y * _sigmoid(y)).astype(o_ref.dtype)


def _conv_prompt(u, conv_w, conv_b, g, bn, batch, seq):
    ts = TOKEN_TILE
    ns = seq // ts
    tile = pl.BlockSpec((ts, D_MODEL), lambda b, j: (b * ns + j, 0))
    const = lambda a: pl.BlockSpec(a.shape, lambda b, j: (0, 0))
    return pl.pallas_call(_conv_prompt_kernel, grid=(batch, ns),
                          in_specs=[tile, const(conv_w), const(conv_b), const(g), const(bn)],
                          out_specs=tile, out_shape=jax.ShapeDtypeStruct(u.shape, BF16),
                          scratch_shapes=[pltpu.VMEM((32 + ts, D_MODEL), F32),
                                          pltpu.VMEM((ts, D_MODEL), F32)],
                          compiler_params=_params("parallel", "arbitrary"),
                          name="conv_prompt")(u, conv_w, conv_b, g, bn)


def _conv_sample_kernel(state_ref, u_ref, w_ref, b_ref, g_ref, bn_ref, o_ref):
    hist = CONV_WIDTH - 1
    y = u_ref[...] * w_ref[hist:hist + 1, :] + b_ref[...]
    for tap in range(hist):
        y = y + state_ref[tap] * w_ref[tap:tap + 1, :]
    y = _layer_norm(y, g_ref[...], bn_ref[...])
    o_ref[...] = (y * _sigmoid(y)).astype(o_ref.dtype)


def _conv_sample(state_t, u, conv_w, conv_b, g, bn):
    full = lambda a: pl.BlockSpec(a.shape, lambda i: (0,) * a.ndim)
    args = (state_t, u, conv_w, conv_b, g, bn)
    return pl.pallas_call(_conv_sample_kernel, grid=(1,), in_specs=[full(a) for a in args],
                          out_specs=full(u), out_shape=jax.ShapeDtypeStruct(u.shape, BF16),
                          compiler_params=_params("arbitrary"), name="conv_sample")(*args)


def _merge_router_kernel(att_ref, conv_ref, ga_ref, gc_ref, x_ref, wa_ref, wc_ref, wo_ref,
                         g1_ref, b1_ref, wr_ref, br_ref, cnt_in_ref,
                         h_ref, idx_ref, wts_ref, rank_ref, cnt_ref):
    tm = x_ref.shape[0]

    @pl.when(pl.program_id(0) == 0)
    def _():
        cnt_ref[...] = cnt_in_ref[...]

    a = _dot(att_ref[...], wa_ref[...])
    c = _dot(conv_ref[...], wc_ref[...])
    mixed = (ga_ref[...] * a + gc_ref[...] * c).astype(BF16)
    res = DEEPNORM_ALPHA * x_ref[...] + _dot(mixed, wo_ref[...])
    h = _layer_norm(res, g1_ref[...], b1_ref[...])
    h_ref[...] = h

    hh, hm, hl = _split3(h)
    e = N_EXPERTS
    by_hi, by_mid, by_lo = _dot(hh, wr_ref[...]), _dot(hm, wr_ref[...]), _dot(hl, wr_ref[...])
    logits = (by_hi[:, 0:e] + by_hi[:, e:2 * e] + by_mid[:, 0:e]
              + by_hi[:, 2 * e:3 * e] + by_mid[:, e:2 * e] + by_lo[:, 0:e]) + br_ref[...]

    eid = lax.broadcasted_iota(jnp.int32, (tm, N_EXPERTS), 1).astype(F32)
    k_lane = lax.broadcasted_iota(jnp.int32, (tm, TOP_K), 1)
    remaining = logits
    chosen = jnp.zeros((tm, N_EXPERTS), F32)
    vals, picks = [], []
    for _ in range(TOP_K):
        mx = jnp.max(remaining, axis=-1, keepdims=True)
        pick = jnp.min(jnp.where(remaining == mx, eid, N_EXPERTS), axis=-1, keepdims=True)
        hit = eid == pick
        chosen = jnp.where(hit, 1.0, chosen)
        remaining = jnp.where(hit, -jnp.inf, remaining)
        vals.append(mx)
        picks.append(pick)
    exps = [jnp.exp(v - vals[0]) for v in vals]
    denom = exps[0] + exps[1] + exps[2] + exps[3]

    rr = lax.broadcasted_iota(jnp.int32, (tm, tm), 0)
    cc = lax.broadcasted_iota(jnp.int32, (tm, tm), 1)
    strict_lower = jnp.where(cc < rr, 1.0, 0.0).astype(BF16)
    rank_dense = _dot(strict_lower, chosen.astype(BF16)) + cnt_ref[...]

    idx_out = jnp.zeros((tm, TOP_K), F32)
    wts_out = jnp.zeros((tm, TOP_K), F32)
    rank_out = jnp.zeros((tm, TOP_K), F32)
    for k in range(TOP_K):
        rk = jnp.sum(jnp.where(eid == picks[k], rank_dense, 0.0), axis=-1, keepdims=True)
        idx_out = jnp.where(k_lane == k, picks[k], idx_out)
        wts_out = jnp.where(k_lane == k, exps[k] / denom, wts_out)
        rank_out = jnp.where(k_lane == k, rk, rank_out)
    idx_ref[...] = idx_out.astype(jnp.int32)
    wts_ref[...] = wts_out
    rank_ref[...] = rank_out.astype(jnp.int32)
    cnt_ref[...] = cnt_ref[...] + jnp.sum(chosen, axis=0, keepdims=True)


def _merge_router(att, conv, ga, gc, x, wa, wc, wo, g1, b1, wr3, br, cnt_in, tm):
    n = x.shape[0]
    row = lambda i: (i, 0)
    tile = pl.BlockSpec((tm, D_MODEL), row)
    small = pl.BlockSpec((tm, TOP_K), row)
    const = lambda a: pl.BlockSpec(a.shape, lambda i: (0,) * a.ndim)
    out_shape = (jax.ShapeDtypeStruct((n, D_MODEL), F32),
                 jax.ShapeDtypeStruct((n, TOP_K), jnp.int32),
                 jax.ShapeDtypeStruct((n, TOP_K), F32),
                 jax.ShapeDtypeStruct((n, TOP_K), jnp.int32),
                 jax.ShapeDtypeStruct((1, N_EXPERTS), F32))
    return pl.pallas_call(
        _merge_router_kernel, grid=(n // tm,),
        in_specs=[tile, tile, tile, tile, tile, const(wa), const(wc), const(wo),
                  const(g1), const(b1), const(wr3), const(br), const(cnt_in)],
        out_specs=(tile, small, small, small, const(cnt_in)), out_shape=out_shape,
        compiler_params=_params("arbitrary"), name="merge_router",
    )(att, conv, ga, gc, x, wa, wc, wo, g1, b1, wr3, br, cnt_in)


def _row_copy(src_ref, src_row, dst_ref, dst_row, sem):
    return pltpu.make_async_copy(src_ref.at[pl.ds(src_row, 1), :], dst_ref.at[pl.ds(dst_row, 1), :], sem)


def _dispatch_kernel(fill_ref, pos_ref, pos_tail_ref, h_ref, h_tail_ref, xs_ref, zero_ref, sem, fill_sem):
    tm = h_ref.shape[0]
    n_tail = h_tail_ref.shape[0]
    i = pl.program_id(0)
    last = pl.num_programs(0) - 1

    @pl.when(i == 0)
    def _():
        zero_ref[...] = jnp.zeros(zero_ref.shape, F32)
        tile = lambda n: pltpu.make_async_copy(
            zero_ref, xs_ref.at[pl.ds(pl.multiple_of(n * ROW_TILE, ROW_TILE), ROW_TILE), :], fill_sem)

        def start(n, carry):
            @pl.when(fill_ref[n] == 1)
            def _():
                tile(n).start()
            return carry

        def finish(n, carry):
            @pl.when(fill_ref[n] == 1)
            def _():
                tile(n).wait()
            return carry

        lax.fori_loop(0, fill_ref.shape[0], start, 0)
        lax.fori_loop(0, fill_ref.shape[0], finish, 0)

    def scatter(src_ref, positions, count):
        def issue(t, carry):
            for k in range(TOP_K):
                _row_copy(src_ref, t, xs_ref, positions[0, 0, t * TOP_K + k], sem).start()
            return carry

        lax.fori_loop(0, count, issue, 0)

    @pl.when(i < last)
    def _():
        scatter(h_ref, pos_ref, tm)
        for _ in range(TOP_K):
            pltpu.make_async_copy(h_ref, xs_ref.at[pl.ds(0, tm), :], sem).wait()

    @pl.when(i == last)
    def _():
        scatter(h_tail_ref, pos_tail_ref, n_tail)
        for _ in range(TOP_K):
            pltpu.make_async_copy(h_tail_ref, xs_ref.at[pl.ds(0, n_tail), :], sem).wait()


def _dispatch(tile_fill, pos, pos_tail, h, h_tail, n_rows, tm):
    n = h.shape[0]
    steps = n // tm
    n_tail = h_tail.shape[0]
    clamp = lambda i, *_: (jnp.minimum(i, steps - 1), 0, 0)
    grid_spec = pltpu.PrefetchScalarGridSpec(
        num_scalar_prefetch=1, grid=(steps + 1,),
        in_specs=[pl.BlockSpec((1, 1, tm * TOP_K), clamp, memory_space=pltpu.SMEM),
                  pl.BlockSpec((1, 1, n_tail * TOP_K), lambda i, *_: (0, 0, 0), memory_space=pltpu.SMEM),
                  pl.BlockSpec((tm, D_MODEL), lambda i, *_: (jnp.minimum(i, steps - 1), 0)),
                  pl.BlockSpec((n_tail, D_MODEL), lambda i, *_: (0, 0))],
        out_specs=pl.BlockSpec(memory_space=pl.ANY),
        scratch_shapes=[pltpu.VMEM((ROW_TILE, D_MODEL), F32), pltpu.SemaphoreType.DMA(()),
                        pltpu.SemaphoreType.DMA(())])
    return pl.pallas_call(
        _dispatch_kernel, grid_spec=grid_spec,
        out_shape=jax.ShapeDtypeStruct((n_rows, D_MODEL), F32),
        compiler_params=_params("arbitrary"), name="moe_dispatch",
    )(tile_fill, pos.reshape(steps, 1, tm * TOP_K), pos_tail.reshape(1, 1, n_tail * TOP_K), h, h_tail)


def _expert_weight_copies(wgu_hbm, wd_hbm, wgu_buf, wd_buf, sems, expert, slot):
    return (pltpu.make_async_copy(wgu_hbm.at[expert], wgu_buf.at[slot], sems.at[0, slot]),
            pltpu.make_async_copy(wd_hbm.at[expert], wd_buf.at[slot], sems.at[1, slot]))


def _experts_kernel(te_ref, tv_ref, slot_ref, nxt_ref, xs_ref, wgu_hbm, bgu_ref, wd_hbm, bd_ref, o_ref,
                    wgu_buf, wd_buf, wgu_bf, wd_bf, sems):
    n = pl.program_id(0)
    prev = te_ref[jnp.maximum(n - 1, 0)]
    copies = functools.partial(_expert_weight_copies, wgu_hbm, wd_hbm, wgu_buf, wd_buf, sems)

    @pl.when(n == 0)
    def _():
        for c in copies(te_ref[0], slot_ref[0]):
            c.start()

    @pl.when((n == 0) | (te_ref[n] != prev))
    def _():
        slot = slot_ref[n]
        for c in copies(te_ref[n], slot):
            c.wait()

        @pl.when(nxt_ref[n] >= 0)
        def _():
            for c in copies(nxt_ref[n], 1 - slot):
                c.start()

        wgu_bf[...] = wgu_buf[slot].astype(BF16)
        wd_bf[...] = wd_buf[slot].astype(BF16)

    @pl.when(tv_ref[n] == 1)
    def _():
        gu = _dot(xs_ref[...].astype(BF16), wgu_bf[...]) + bgu_ref[0]
        gate = jnp.minimum(gu[:, :D_FF], SWIGLU_LIMIT)
        up = jnp.clip(gu[:, D_FF:], -SWIGLU_LIMIT, SWIGLU_LIMIT)
        act = (up + 1.0) * gate * _sigmoid(SWIGLU_ALPHA * gate)
        o_ref[...] = _dot(act.astype(BF16), wd_bf[...]) + bd_ref[0]

    @pl.when(tv_ref[n] == 0)
    def _():
        o_ref[...] = jnp.zeros(o_ref.shape, F32)


def _experts(tile_expert, tile_valid, tile_slot, next_expert, xs, w_gate_up, b_gate_up, w_down, b_down):
    rows = xs.shape[0]
    tm = ROW_TILE
    e3 = lambda n, te, *_: (te[n], 0, 0)
    row = lambda n, *_: (n, 0)
    grid_spec = pltpu.PrefetchScalarGridSpec(
        num_scalar_prefetch=4, grid=(rows // tm,),
        in_specs=[pl.BlockSpec((tm, D_MODEL), row),
                  pl.BlockSpec(memory_space=pl.ANY),
                  pl.BlockSpec((1, 1, 2 * D_FF), e3),
                  pl.BlockSpec(memory_space=pl.ANY),
                  pl.BlockSpec((1, 1, D_MODEL), e3)],
        out_specs=pl.BlockSpec((tm, D_MODEL), row),
        scratch_shapes=[pltpu.VMEM((2, D_MODEL, 2 * D_FF), F32), pltpu.VMEM((2, D_FF, D_MODEL), F32),
                        pltpu.VMEM((D_MODEL, 2 * D_FF), BF16), pltpu.VMEM((D_FF, D_MODEL), BF16),
                        pltpu.SemaphoreType.DMA((2, 2))])
    return pl.pallas_call(_experts_kernel, grid_spec=grid_spec,
                          out_shape=jax.ShapeDtypeStruct((rows, D_MODEL), F32),
                          compiler_params=_params("arbitrary"), name="moe_experts",
                          )(tile_expert, tile_valid, tile_slot, next_expert, xs, w_gate_up,
                            b_gate_up.reshape(N_EXPERTS, 1, 2 * D_FF), w_down,
                            b_down.reshape(N_EXPERTS, 1, D_MODEL))


def _combine_kernel(pos_ref, pos_next_ref, wts_ref, h_ref, ys_ref, g_ref, b_ref, o_ref, buf_ref, sems):
    tm = h_ref.shape[0]
    i = pl.program_id(0)
    slot = i % 2

    def gather(positions, dst_slot):
        def issue(t, carry):
            for k in range(TOP_K):
                _row_copy(ys_ref, positions[0, 0, t * TOP_K + k], buf_ref.at[dst_slot, k], t,
                          sems.at[dst_slot]).start()
            return carry

        lax.fori_loop(0, tm, issue, 0)

    @pl.when(i == 0)
    def _():
        gather(pos_ref, slot)

    @pl.when(i + 1 < pl.num_programs(0))
    def _():
        gather(pos_next_ref, 1 - slot)

    for k in range(TOP_K):
        pltpu.make_async_copy(ys_ref.at[pl.ds(0, tm), :], buf_ref.at[slot, k], sems.at[slot]).wait()
    wts = wts_ref[...]
    moe = wts[:, 0:1] * buf_ref[slot, 0]
    for k in range(1, TOP_K):
        moe = moe + wts[:, k:k + 1] * buf_ref[slot, k]
    o_ref[...] = _layer_norm(DEEPNORM_ALPHA * h_ref[...] + moe, g_ref[...], b_ref[...])


def _combine(pos, wts, h, ys, g2, b2, tm):
    n = h.shape[0]
    pos3 = pos.reshape(n // tm, 1, tm * TOP_K)
    steps = n // tm
    row = lambda i: (i, 0)
    const = lambda a: pl.BlockSpec(a.shape, lambda i: (0,) * a.ndim)
    pos_block = lambda index: pl.BlockSpec((1, 1, tm * TOP_K), index, memory_space=pltpu.SMEM)
    return pl.pallas_call(
        _combine_kernel, grid=(steps,),
        in_specs=[pos_block(lambda i: (i, 0, 0)),
                  pos_block(lambda i: (jnp.minimum(i + 1, steps - 1), 0, 0)),
                  pl.BlockSpec((tm, TOP_K), row),
                  pl.BlockSpec((tm, D_MODEL), row),
                  pl.BlockSpec(memory_space=pl.ANY), const(g2), const(b2)],
        out_specs=pl.BlockSpec((tm, D_MODEL), row),
        out_shape=jax.ShapeDtypeStruct((n, D_MODEL), F32),
        scratch_shapes=[pltpu.VMEM((2, TOP_K, tm, D_MODEL), F32), pltpu.SemaphoreType.DMA((2,))],
        compiler_params=_params("arbitrary"), name="moe_combine")(pos3, pos3, wts, h, ys, g2, b2)


def _split_in_proj(w_in, b_forget):
    a = N_HEADS * HEAD_DIM
    cuts = [0, a, 2 * a, 3 * a]
    f0 = 3 * a
    rest = f0 + N_HEADS
    starts = cuts[:3] + [rest + i * D_MODEL for i in range(4)]
    w_main = jnp.stack([w_in[:, s:s + D_MODEL] for s in starts]).astype(BF16)
    w_t = jnp.stack([w_in[:, s:s + a].T for s in cuts[1:3]]).astype(BF16)
    w_f = w_in[:, f0:rest].astype(BF16)
    return w_main, w_t, w_f, w_f.T, b_forget.reshape(1, N_HEADS), b_forget.reshape(N_HEADS, 1)


def _routing_tables(counts, n_tiles):
    cnt = counts.reshape(N_EXPERTS).astype(jnp.int32)
    tiles = (cnt + ROW_TILE - 1) // ROW_TILE
    tile_end = jnp.cumsum(tiles)
    start_row = (tile_end - tiles) * ROW_TILE
    n = jnp.arange(n_tiles, dtype=jnp.int32)
    valid = n < tile_end[-1]
    owner = jnp.sum((n[:, None] >= tile_end[None, :]).astype(jnp.int32), axis=1)
    last_owner = jnp.sum((tile_end[-1] - 1 >= tile_end).astype(jnp.int32))
    tile_expert = jnp.where(valid, owner, last_owner).astype(jnp.int32)
    experts = jnp.arange(N_EXPERTS, dtype=jnp.int32)[None, :]
    later = (tiles[None, :] > 0) & (experts > tile_expert[:, None])
    earlier = (tiles[None, :] > 0) & (experts < tile_expert[:, None])
    next_expert = jnp.min(jnp.where(later, experts, N_EXPERTS), axis=1)
    next_expert = jnp.where(next_expert == N_EXPERTS, -1, next_expert).astype(jnp.int32)
    tile_slot = (jnp.sum(earlier.astype(jnp.int32), axis=1) % 2).astype(jnp.int32)
    ends_group = jnp.any((n[:, None] + 1 == tile_end[None, :]) & (tiles[None, :] > 0), axis=1)
    tile_fill = (ends_group | ~valid).astype(jnp.int32)
    return start_row, tile_expert, valid.astype(jnp.int32), tile_slot, next_expert, tile_fill


def _layer(xp, xs, cache_k, cache_v, cache_logf, state_conv, page_table,
           w_in, b_forget, conv_w, conv_b, conv_norm_g, conv_norm_b,
           w_attn_proj, w_conv_proj, w_out, ln1_g, ln1_b,
           w_router, b_router, w_gate_up, b_gate_up, w_down, b_down, ln2_g, ln2_b):
    batch, seq, d = xp.shape
    n_req = xs.shape[0]
    n_p = batch * seq
    row = lambda a: a.reshape(1, -1)
    heads = lambda a: a.reshape(a.shape[0], N_HEADS, HEAD_DIM)

    w_main, w_t, w_f, w_ft, b_f, b_ft = _split_in_proj(w_in, b_forget)
    wa, wc, wo = (w.astype(BF16) for w in (w_attn_proj, w_conv_proj, w_out))
    wr3 = jnp.concatenate(_split3(w_router), axis=1)
    cw, cb, cg, cbn = conv_w, row(conv_b), row(conv_norm_g), row(conv_norm_b)
    g1, b1, g2, b2, br = row(ln1_g), row(ln1_b), row(ln2_g), row(ln2_b), row(b_router)

    xp2 = xp.reshape(n_p, d)
    qb, u_p, ga_p, gc_p, kt_p, vt_p, lft_p, ke, ve = _in_projection(
        xp2, w_main, w_f, w_ft, b_f, b_ft, TOKEN_TILE, w_t, seq)
    att_p = _prompt_attention(qb, ke, ve, batch, seq)
    conv_p = _conv_prompt(u_p, cw, cb, cg, cbn, batch, seq)

    xs2 = xs.reshape(n_req, d)
    q_s, u_s, ga_s, gc_s, k_s, v_s, lf_s, lft_s = _in_projection(
        xs2, w_main, w_f, w_ft, b_f, b_ft, n_req)
    att_s = _sample_attention(heads(q_s), heads(k_s), heads(v_s), lft_s,
                              jnp.transpose(cache_k, (0, 2, 3, 1)), jnp.transpose(cache_v, (0, 2, 3, 1)),
                              jnp.transpose(cache_logf, (0, 2, 1)), page_table)
    state_t = jnp.transpose(state_conv, (1, 0, 2))
    conv_s = _conv_sample(state_t, u_s, cw, cb, cg, cbn)

    zero_cnt = jnp.zeros((1, N_EXPERTS), F32)
    h_p, idx_p, wts_p, rank_p, cnt_p = _merge_router(
        att_p, conv_p, ga_p, gc_p, xp2, wa, wc, wo, g1, b1, wr3, br, zero_cnt, TOKEN_TILE)
    h_s, idx_s, wts_s, rank_s, cnt = _merge_router(
        att_s, conv_s, ga_s, gc_s, xs2, wa, wc, wo, g1, b1, wr3, br, cnt_p, n_req)

    n_tok = n_p + n_req
    n_tiles = (n_tok * TOP_K + N_EXPERTS * (ROW_TILE - 1) + ROW_TILE - 1) // ROW_TILE
    start_row, tile_expert, tile_valid, tile_slot, next_expert, tile_fill = _routing_tables(cnt, n_tiles)
    expert_ids = jnp.arange(N_EXPERTS, dtype=jnp.int32)
    slot_of = lambda idx, rank: rank + jnp.sum(
        jnp.where(idx[..., None] == expert_ids, start_row, 0), axis=-1)
    pos_p = slot_of(idx_p, rank_p)
    pos_s = slot_of(idx_s, rank_s)
    sorted_rows = _dispatch(tile_fill, pos_p, pos_s, h_p, h_s, n_tiles * ROW_TILE, TOKEN_TILE)
    expert_out = _experts(tile_expert, tile_valid, tile_slot, next_expert, sorted_rows,
                          w_gate_up, b_gate_up, w_down, b_down)
    y_p = _combine(pos_p, wts_p, h_p, expert_out, g2, b2, TOKEN_TILE)
    y_s = _combine(pos_s, wts_s, h_s, expert_out, g2, b2, n_req)

    hist = CONV_WIDTH - 1
    token_major = lambda t: jnp.transpose(t.reshape(batch, N_HEADS, HEAD_DIM, seq), (0, 3, 1, 2))
    conv_state_p = u_p.reshape(batch, seq, d)[:, seq - hist:, :]
    conv_state_s = jnp.transpose(jnp.concatenate([state_t[1:], u_s[None]], axis=0), (1, 0, 2))
    return (y_p.reshape(batch, seq, d), y_s.reshape(n_req, 1, d),
            token_major(kt_p), token_major(vt_p), jnp.transpose(lft_p, (0, 2, 1)), conv_state_p,
            k_s.reshape(n_req, 1, N_HEADS, HEAD_DIM), v_s.reshape(n_req, 1, N_HEADS, HEAD_DIM),
            lf_s.reshape(n_req, 1, N_HEADS), conv_state_s)


def kernel(x_prompt, x_sample, cache_k, cache_v, cache_logf, state_conv, page_table, w_in, b_forget, conv_w, conv_b, conv_norm_g, conv_norm_b, w_attn_proj, w_conv_proj, w_out, ln1_g, ln1_b, w_router, b_router, w_gate_up, b_gate_up, w_down, b_down, ln2_g, ln2_b):
    assert x_prompt.shape[-1] == D_MODEL and w_in.shape[0] == DEPTH
    out = _layer(x_prompt, x_sample, cache_k[0], cache_v[0], cache_logf[0], state_conv[0], page_table,
                 w_in[0], b_forget[0], conv_w[0], conv_b[0], conv_norm_g[0], conv_norm_b[0],
                 w_attn_proj[0], w_conv_proj[0], w_out[0], ln1_g[0], ln1_b[0],
                 w_router[0], b_router[0], w_gate_up[0], b_gate_up[0], w_down[0], b_down[0],
                 ln2_g[0], ln2_b[0])
    y_p, y_s = out[0], out[1]
    return (y_p, y_s) + tuple(o[None] for o in out[2:])
```

```python
import functools
import math

import jax
import jax.numpy as jnp
from jax import lax
from jax.experimental import pallas as pl
from jax.experimental.pallas import tpu as pltpu

F32 = jnp.float32
BF16 = jnp.bfloat16

D_MODEL = 1024
N_HEADS = 16
HEAD_DIM = 64
CONV_WIDTH = 31
N_EXPERTS = 32
TOP_K = 4
D_FF = 1024
PAGE_SIZE = 128
SWIGLU_ALPHA = 1.702
SWIGLU_LIMIT = 7.0
LN_EPS = 1e-5
DEPTH = 1
DEEPNORM_ALPHA = (2 * DEPTH) ** 0.25
LOG2E = math.log2(math.e)

LANES = 128
SUBLANES = 8
HEADS_PER_LANE_TILE = LANES // HEAD_DIM
VMEM_LIMIT = 56 * 2 ** 20

TOKEN_TILE = 256
ATTN_TILE = 512
ROW_TILE = 256
PAGES_PER_STEP = 8
N_BIAS = 3
KEY_EXT = N_HEADS * LANES


def _params(*sem):
    return pltpu.CompilerParams(dimension_semantics=sem, vmem_limit_bytes=VMEM_LIMIT)


def _split3(x):
    hi = x.astype(BF16)
    r = x - hi.astype(F32)
    mid = r.astype(BF16)
    lo = (r - mid.astype(F32)).astype(BF16)
    return hi, mid, lo


def _dot(a, b):
    return jnp.dot(a, b, preferred_element_type=F32)


def _cumsum_lanes(x, upper, carry):
    hi, mid, lo = _split3(x)
    return _dot(hi, upper) + _dot(mid, upper) + _dot(lo, upper) + carry


def _upper_ones(n):
    rr = lax.broadcasted_iota(jnp.int32, (n, n), 0)
    cc = lax.broadcasted_iota(jnp.int32, (n, n), 1)
    return jnp.where(rr <= cc, 1.0, 0.0).astype(BF16)


def _log_sigmoid(x):
    return jnp.minimum(x, 0.0) - jnp.log1p(jnp.exp(-jnp.abs(x)))


def _sigmoid(x):
    return 1.0 / (1.0 + jnp.exp(-x))


def _layer_norm(x, g, b):
    mu = jnp.mean(x, axis=-1, keepdims=True)
    xc = x - mu
    var = jnp.mean(xc * xc, axis=-1, keepdims=True)
    return xc * lax.rsqrt(var + LN_EPS) * g + b


def _tree(xs, op):
    xs = list(xs)
    while len(xs) > 1:
        xs = [op(xs[i], xs[i + 1]) if i + 1 < len(xs) else xs[i] for i in range(0, len(xs), 2)]
    return xs[0]


def _extended_rows(rows, extra, head):
    pad = jnp.zeros((HEAD_DIM - SUBLANES, rows.shape[1]), F32)
    return [rows, extra, pad] if head % 2 == 0 else [extra, pad, rows]


CONV_HALO = 32


def _conv_tile(u, buf_ref, y_ref, w_ref, b_ref, g_ref, bn_ref):
    ts = u.shape[0]
    halo = CONV_HALO
    buf_ref[0:halo, :] = buf_ref[ts:ts + halo, :]
    buf_ref[halo:halo + ts, :] = u
    first_off = halo - (CONV_WIDTH - 1)
    for c in range(D_MODEL // LANES):
        sl = slice(c * LANES, (c + 1) * LANES)
        y = None
        for r in range(SUBLANES):
            rows = ts if r == 0 else ts + SUBLANES
            z = None
            for a in range((halo + SUBLANES) // SUBLANES):
                tap = a * SUBLANES + r - first_off
                if 0 <= tap < CONV_WIDTH:
                    term = buf_ref[a * SUBLANES:a * SUBLANES + rows, sl] * w_ref[tap:tap + 1, sl]
                    z = term if z is None else z + term
            part = z if r == 0 else z[r:r + ts]
            y = part if y is None else y + part
        y_ref[:, sl] = y + b_ref[:, sl]
    y = _layer_norm(y_ref[...], g_ref[...], bn_ref[...])
    return (y * _sigmoid(y)).astype(BF16)


def _inproj_kernel(*refs, emit_keys, tiles_per_seq):
    if emit_keys:
        (x_ref, w_ref, wf_ref, wft_ref, bf_ref, bft_ref, wt_ref, cw_ref, cb_ref, cg_ref, cbn_ref,
         q_ref, u_ref, ga_ref, gc_ref, kt_ref, vt_ref, lft_ref, ke_ref, ve_ref, conv_ref,
         carry_ref, buf_ref, y_ref) = refs
    else:
        (x_ref, w_ref, wf_ref, wft_ref, bf_ref, bft_ref,
         q_ref, u_ref, ga_ref, gc_ref, k_ref, v_ref, lf_ref, lft_ref) = refs
    tm = x_ref.shape[0]
    nt = (((1,), (1,)), ((), ()))
    x = x_ref[...].astype(BF16)
    q = _dot(x, w_ref[0])
    q_ref[...] = (q * (LOG2E * HEAD_DIM ** -0.5)).astype(BF16)
    u = _dot(x, w_ref[3]) * _sigmoid(_dot(x, w_ref[4]))
    u_ref[...] = u
    if emit_keys:
        @pl.when(pl.program_id(0) % tiles_per_seq == 0)
        def _():
            carry_ref[...] = jnp.zeros(carry_ref.shape, F32)
            buf_ref[tm:tm + CONV_HALO, :] = jnp.zeros((CONV_HALO, D_MODEL), F32)

        conv_ref[...] = _conv_tile(u, buf_ref, y_ref, cw_ref, cb_ref, cg_ref, cbn_ref)
    ga_ref[...] = _sigmoid(_dot(x, w_ref[5]))
    gc_ref[...] = _sigmoid(_dot(x, w_ref[6]))
    ft = lax.dot_general(wft_ref[...], x, nt, preferred_element_type=F32)
    lft = _log_sigmoid(ft + bft_ref[...])
    if not emit_keys:
        k_ref[...] = _dot(x, w_ref[1])
        v_ref[...] = _dot(x, w_ref[2])
        lf_ref[...] = _log_sigmoid(_dot(x, wf_ref[...]) + bf_ref[...])
        lft_ref[...] = lft
        return

    kt = lax.dot_general(wt_ref[0], x, nt, preferred_element_type=F32)
    vt = lax.dot_general(wt_ref[1], x, nt, preferred_element_type=F32)
    kt_ref[0] = kt
    vt_ref[0] = vt
    lft_ref[0] = lft

    csum = _cumsum_lanes(lft * LOG2E, _upper_ones(tm), carry_ref[...])
    carry_ref[...] = csum[:, tm - 1:tm]
    pieces = [p.astype(F32) for p in _split3(-csum)]
    sub = lax.broadcasted_iota(jnp.int32, (SUBLANES, tm), 0)
    ones_rows = jnp.where(sub == 0, 1.0, 0.0)
    k_blocks, v_blocks = [], []
    for h in range(N_HEADS):
        bias_rows = jnp.zeros((SUBLANES, tm), F32)
        for i, piece in enumerate(pieces):
            bias_rows = jnp.where(sub == i, piece[h:h + 1, :], bias_rows)
        feat = slice(h * HEAD_DIM, (h + 1) * HEAD_DIM)
        k_blocks += _extended_rows(kt[feat], bias_rows, h)
        v_blocks += _extended_rows(vt[feat], ones_rows, h)
    ke_ref[0] = jnp.concatenate(k_blocks, axis=0).astype(BF16)
    ve_ref[0] = jnp.concatenate(v_blocks, axis=0).astype(BF16)


def _in_projection(x, w_main, w_f, w_ft, b_f, b_ft, tm, w_t=None, conv=None, seq=None):
    emit_keys = seq is not None
    n = x.shape[0]
    row = lambda i: (i, 0)
    const = lambda a: pl.BlockSpec(a.shape, lambda i: (0,) * a.ndim)
    once = lambda a: pl.BlockSpec(a.shape, lambda i: (0,) * a.ndim, pipeline_mode=pl.Buffered(1))
    big = lambda dt: jax.ShapeDtypeStruct((n, D_MODEL), dt)
    tile = pl.BlockSpec((tm, D_MODEL), row)
    args = [x, w_main, w_f, w_ft, b_f, b_ft]
    in_specs = [tile, once(w_main), const(w_f), const(w_ft), const(b_f), const(b_ft)]
    out_shape = [big(BF16), big(F32), big(F32), big(F32)]
    out_specs = [tile, tile, tile, tile]
    scratch = []
    if emit_keys:
        tiles_per_seq = seq // tm
        batch = n // seq
        seq_major = lambda rows, dt: jax.ShapeDtypeStruct((batch, rows, seq), dt)
        seq_tile = lambda rows: pl.BlockSpec(
            (1, rows, tm), lambda i: (i // tiles_per_seq, 0, i % tiles_per_seq))
        args += [w_t, *conv]
        in_specs += [once(w_t)] + [const(a) for a in conv]
        out_shape += [seq_major(D_MODEL, F32), seq_major(D_MODEL, F32), seq_major(N_HEADS, F32),
                      seq_major(KEY_EXT, BF16), seq_major(KEY_EXT, BF16), big(BF16)]
        out_specs += [seq_tile(D_MODEL), seq_tile(D_MODEL), seq_tile(N_HEADS),
                      seq_tile(KEY_EXT), seq_tile(KEY_EXT), tile]
        scratch = [pltpu.VMEM((N_HEADS, 1), F32), pltpu.VMEM((CONV_HALO + tm, D_MODEL), F32),
                   pltpu.VMEM((tm, D_MODEL), F32)]
    else:
        tiles_per_seq = 1
        out_shape += [big(F32), big(F32), jax.ShapeDtypeStruct((n, N_HEADS), F32),
                      jax.ShapeDtypeStruct((N_HEADS, n), F32)]
        out_specs += [tile, tile, pl.BlockSpec((tm, N_HEADS), row),
                      pl.BlockSpec((N_HEADS, tm), lambda i: (0, i))]
    body = functools.partial(_inproj_kernel, emit_keys=emit_keys, tiles_per_seq=tiles_per_seq)
    return pl.pallas_call(body, grid=(n // tm,), in_specs=in_specs, out_specs=out_specs,
                          out_shape=out_shape, scratch_shapes=scratch,
                          compiler_params=_params("arbitrary"), name="in_projection")(*args)


def _flash_kernel(q_ref, ke_ref, ve_ref, o_ref):
    t = ATTN_TILE
    i = pl.program_id(2)
    lane = lax.broadcasted_iota(jnp.int32, (1, LANES), 1)
    first = lane < HEAD_DIM
    ones_at = lambda cond: jnp.where(cond, 1.0, 0.0).astype(BF16)
    q2 = q_ref[...]
    q_ext = (jnp.where(first, q2, ones_at(lane < HEAD_DIM + N_BIAS)),
             jnp.where(first, ones_at(lane < N_BIAS), q2))
    den_lane = (HEAD_DIM, 0)
    nt = (((1,), (1,)), ((), ()))

    def step(j, carry, masked):
        ks = pl.multiple_of(j * t, t)
        out = []
        for h in range(HEADS_PER_LANE_TILE):
            m, acc = carry[2 * h], carry[2 * h + 1]
            rows = slice(h * LANES, (h + 1) * LANES)
            s = _dot(q_ext[h], ke_ref[0, rows, pl.ds(ks, t)])
            if masked:
                rr = lax.broadcasted_iota(jnp.int32, (t, t), 0)
                cc = lax.broadcasted_iota(jnp.int32, (t, t), 1)
                s = jnp.where(cc <= rr, s, -jnp.inf)
            m_new = jnp.maximum(m, jnp.max(s, axis=-1, keepdims=True))
            alpha = jnp.exp2(m - m_new)
            p = jnp.exp2(s - m_new).astype(BF16)
            pv = lax.dot_general(p, ve_ref[0, rows, pl.ds(ks, t)], nt, preferred_element_type=F32)
            out += [m_new, alpha * acc + pv]
        return tuple(out)

    neg = jnp.full((t, 1), -jnp.inf, F32)
    za = jnp.zeros((t, LANES), F32)
    carry = lax.fori_loop(0, i, lambda j, c: step(j, c, False), (neg, za, neg, za))
    _, acc0, _, acc1 = step(i, carry, True)
    inv0 = 1.0 / acc0[:, den_lane[0]:den_lane[0] + 1]
    inv1 = 1.0 / acc1[:, den_lane[1]:den_lane[1] + 1]
    o_ref[...] = jnp.where(first, acc0 * inv0, acc1 * inv1).astype(o_ref.dtype)


def _prompt_attention(qb, ke, ve, batch, seq):
    t = ATTN_TILE
    nq = seq // t
    pairs = N_HEADS // HEADS_PER_LANE_TILE
    q_spec = pl.BlockSpec((t, LANES), lambda b, hp, i: (b * nq + i, hp))
    e_spec = pl.BlockSpec((1, HEADS_PER_LANE_TILE * LANES, seq), lambda b, hp, i: (b, hp, 0))
    return pl.pallas_call(_flash_kernel, grid=(batch, pairs, nq),
                          in_specs=[q_spec, e_spec, e_spec], out_specs=q_spec,
                          out_shape=jax.ShapeDtypeStruct(qb.shape, BF16),
                          compiler_params=_params("parallel", "parallel", "arbitrary"),
                          name="prompt_attention")(qb, ke, ve)


def _paged_kernel(pt_ref, *refs):
    pp = PAGES_PER_STEP
    k_refs, v_refs, lf_refs = refs[:pp], refs[pp:2 * pp], refs[2 * pp:3 * pp]
    qrep_ref, q_ref, kn_ref, vn_ref, lfnt_ref, o_ref, m_ref, l_ref, c_ref, acc_ref = refs[3 * pp:]
    r = pl.program_id(0)
    g = pl.program_id(1)
    wide = (N_HEADS, PAGE_SIZE)

    @pl.when(g == 0)
    def _():
        m_ref[...] = jnp.full(m_ref.shape, -jnp.inf, F32)
        l_ref[...] = jnp.zeros(l_ref.shape, F32)
        c_ref[...] = jnp.zeros(c_ref.shape, F32)
        acc_ref[...] = jnp.zeros(acc_ref.shape, F32)

    upper = _upper_ones(PAGE_SIZE)
    carry = c_ref[...]
    logits = []
    for p_i in range(pp):
        incl = _cumsum_lanes(lf_refs[p_i][0] * LOG2E, upper, carry)
        carry = jnp.broadcast_to(incl[:, PAGE_SIZE - 1:PAGE_SIZE], wide)
        qk = [jnp.sum(k_refs[p_i][0, h] * qrep_ref[0, h], axis=0, keepdims=True) for h in range(N_HEADS)]
        logits.append(jnp.concatenate(qk, axis=0) - incl)
    c_ref[...] = carry
    m_old = m_ref[...]
    step_max = jnp.max(_tree(logits, jnp.maximum), axis=-1, keepdims=True)
    m_new = jnp.maximum(m_old, jnp.broadcast_to(step_max, wide))
    m_ref[...] = m_new
    alpha = jnp.exp2(m_old - m_new)
    probs = [jnp.exp2(lg - m_new) for lg in logits]
    l_ref[...] = alpha * l_ref[...] + _tree(probs, jnp.add)
    for h in range(N_HEADS):
        head = slice(h, h + 1)
        pv = _tree([probs[p_i][head, :] * v_refs[p_i][0, h] for p_i in range(pp)], jnp.add)
        acc_ref[h] = acc_ref[h] * alpha[head, :] + pv

    @pl.when(g == pl.num_programs(1) - 1)
    def _():
        q = q_ref[0].astype(F32)
        s_new = jnp.sum(kn_ref[0] * q, axis=-1, keepdims=True)
        lane = lax.broadcasted_iota(jnp.int32, lfnt_ref.shape, 1)
        lf_new = jnp.sum(jnp.where(lane == r, lfnt_ref[...], 0.0), axis=-1, keepdims=True)
        logit = s_new - (c_ref[:, 0:1] + lf_new * LOG2E)
        m_last = m_ref[:, 0:1]
        m_fin = jnp.maximum(m_last, logit)
        a_fin = jnp.exp2(m_last - m_fin)
        p_new = jnp.exp2(logit - m_fin)
        l_fin = a_fin * jnp.sum(l_ref[...], axis=-1, keepdims=True) + p_new
        for h in range(N_HEADS):
            head = slice(h, h + 1)
            tot = jnp.sum(acc_ref[h], axis=-1, keepdims=True)
            o_ref[0, h] = (a_fin[head] * tot + p_new[head] * vn_ref[0, h]) / l_fin[head]


def _sample_attention(q_s, k_s, v_s, lft_s, cache_kt, cache_vt, cache_lft, page_table):
    n_req, n_pages = page_table.shape
    pp = PAGES_PER_STEP
    steps = n_pages // pp
    q_rep = jnp.broadcast_to(q_s.astype(F32)[..., None], q_s.shape + (PAGE_SIZE,))

    def page_map(p_i, nd):
        return lambda r, g, pt: (pt[r * n_pages + g * pp + p_i],) + (0,) * nd

    k_specs = [pl.BlockSpec((1, N_HEADS, HEAD_DIM, PAGE_SIZE), page_map(p, 3)) for p in range(pp)]
    lf_specs = [pl.BlockSpec((1, N_HEADS, PAGE_SIZE), page_map(p, 2)) for p in range(pp)]
    req3 = lambda r, g, pt: (r, 0, 0)
    req4 = lambda r, g, pt: (r, 0, 0, 0)
    head = pl.BlockSpec((1, N_HEADS, HEAD_DIM), req3)
    column = pl.BlockSpec((1, N_HEADS, HEAD_DIM, 1), req4)
    wide = lambda: pltpu.VMEM((N_HEADS, PAGE_SIZE), F32)
    grid_spec = pltpu.PrefetchScalarGridSpec(
        num_scalar_prefetch=1, grid=(n_req, steps),
        in_specs=k_specs + k_specs + lf_specs
        + [pl.BlockSpec((1, N_HEADS, HEAD_DIM, PAGE_SIZE), req4), head, head, column,
           pl.BlockSpec(lft_s.shape, lambda r, g, pt: (0, 0))],
        out_specs=column,
        scratch_shapes=[wide(), wide(), wide(), pltpu.VMEM((N_HEADS, HEAD_DIM, PAGE_SIZE), F32)])
    args = [cache_kt] * pp + [cache_vt] * pp + [cache_lft] * pp
    args += [q_rep, q_s, k_s, v_s[..., None], lft_s]
    out = pl.pallas_call(_paged_kernel, grid_spec=grid_spec,
                         out_shape=jax.ShapeDtypeStruct((n_req, N_HEADS, HEAD_DIM, 1), F32),
                         compiler_params=_params("arbitrary", "arbitrary"),
                         name="sample_attention")(page_table.reshape(-1), *args)
    return out.reshape(n_req, N_HEADS * HEAD_DIM).astype(BF16)


def _conv_sample_kernel(state_ref, u_ref, w_ref, b_ref, g_ref, bn_ref, o_ref):
    hist = CONV_WIDTH - 1
    y = u_ref[...] * w_ref[hist:hist + 1, :] + b_ref[...]
    for tap in range(hist):
        y = y + state_ref[tap] * w_ref[tap:tap + 1, :]
    y = _layer_norm(y, g_ref[...], bn_ref[...])
    o_ref[...] = (y * _sigmoid(y)).astype(o_ref.dtype)


def _conv_sample(state_t, u, conv_w, conv_b, g, bn):
    full = lambda a: pl.BlockSpec(a.shape, lambda i: (0,) * a.ndim)
    args = (state_t, u, conv_w, conv_b, g, bn)
    return pl.pallas_call(_conv_sample_kernel, grid=(1,), in_specs=[full(a) for a in args],
                          out_specs=full(u), out_shape=jax.ShapeDtypeStruct(u.shape, BF16),
                          compiler_params=_params("arbitrary"), name="conv_sample")(*args)


def _merge_router_kernel(att_ref, conv_ref, ga_ref, gc_ref, x_ref, wa_ref, wc_ref, wo_ref,
                         g1_ref, b1_ref, wr_ref, br_ref, cnt_in_ref,
                         h_ref, idx_ref, wts_ref, rank_ref, cnt_ref):
    tm = x_ref.shape[0]

    @pl.when(pl.program_id(0) == 0)
    def _():
        cnt_ref[...] = cnt_in_ref[...]

    a = _dot(att_ref[...], wa_ref[...])
    c = _dot(conv_ref[...], wc_ref[...])
    mixed = (ga_ref[...] * a + gc_ref[...] * c).astype(BF16)
    res = DEEPNORM_ALPHA * x_ref[...] + _dot(mixed, wo_ref[...])
    h = _layer_norm(res, g1_ref[...], b1_ref[...])
    h_ref[...] = h

    hh, hm, hl = _split3(h)
    e = N_EXPERTS
    by_hi, by_mid, by_lo = _dot(hh, wr_ref[...]), _dot(hm, wr_ref[...]), _dot(hl, wr_ref[...])
    logits = (by_hi[:, 0:e] + by_hi[:, e:2 * e] + by_mid[:, 0:e]
              + by_hi[:, 2 * e:3 * e] + by_mid[:, e:2 * e] + by_lo[:, 0:e]) + br_ref[...]

    eid = lax.broadcasted_iota(jnp.int32, (tm, N_EXPERTS), 1).astype(F32)
    k_lane = lax.broadcasted_iota(jnp.int32, (tm, TOP_K), 1)
    remaining = logits
    chosen = jnp.zeros((tm, N_EXPERTS), F32)
    vals, picks = [], []
    for _ in range(TOP_K):
        mx = jnp.max(remaining, axis=-1, keepdims=True)
        pick = jnp.min(jnp.where(remaining == mx, eid, N_EXPERTS), axis=-1, keepdims=True)
        hit = eid == pick
        chosen = jnp.where(hit, 1.0, chosen)
        remaining = jnp.where(hit, -jnp.inf, remaining)
        vals.append(mx)
        picks.append(pick)
    exps = [jnp.exp(v - vals[0]) for v in vals]
    denom = exps[0] + exps[1] + exps[2] + exps[3]

    rr = lax.broadcasted_iota(jnp.int32, (tm, tm), 0)
    cc = lax.broadcasted_iota(jnp.int32, (tm, tm), 1)
    strict_lower = jnp.where(cc < rr, 1.0, 0.0).astype(BF16)
    rank_dense = _dot(strict_lower, chosen.astype(BF16)) + cnt_ref[...]

    idx_out = jnp.zeros((tm, TOP_K), F32)
    wts_out = jnp.zeros((tm, TOP_K), F32)
    rank_out = jnp.zeros((tm, TOP_K), F32)
    for k in range(TOP_K):
        rk = jnp.sum(jnp.where(eid == picks[k], rank_dense, 0.0), axis=-1, keepdims=True)
        idx_out = jnp.where(k_lane == k, picks[k], idx_out)
        wts_out = jnp.where(k_lane == k, exps[k] / denom, wts_out)
        rank_out = jnp.where(k_lane == k, rk, rank_out)
    idx_ref[...] = idx_out.astype(jnp.int32)
    wts_ref[...] = wts_out
    rank_ref[...] = rank_out.astype(jnp.int32)
    cnt_ref[...] = cnt_ref[...] + jnp.sum(chosen, axis=0, keepdims=True)


def _merge_router(att, conv, ga, gc, x, wa, wc, wo, g1, b1, wr3, br, cnt_in, tm):
    n = x.shape[0]
    row = lambda i: (i, 0)
    tile = pl.BlockSpec((tm, D_MODEL), row)
    small = pl.BlockSpec((tm, TOP_K), row)
    const = lambda a: pl.BlockSpec(a.shape, lambda i: (0,) * a.ndim)
    out_shape = (jax.ShapeDtypeStruct((n, D_MODEL), F32),
                 jax.ShapeDtypeStruct((n, TOP_K), jnp.int32),
                 jax.ShapeDtypeStruct((n, TOP_K), F32),
                 jax.ShapeDtypeStruct((n, TOP_K), jnp.int32),
                 jax.ShapeDtypeStruct((1, N_EXPERTS), F32))
    return pl.pallas_call(
        _merge_router_kernel, grid=(n // tm,),
        in_specs=[tile, tile, tile, tile, tile, const(wa), const(wc), const(wo),
                  const(g1), const(b1), const(wr3), const(br), const(cnt_in)],
        out_specs=(tile, small, small, small, const(cnt_in)), out_shape=out_shape,
        compiler_params=_params("arbitrary"), name="merge_router",
    )(att, conv, ga, gc, x, wa, wc, wo, g1, b1, wr3, br, cnt_in)


def _row_copy(src_ref, src_row, dst_ref, dst_row, sem):
    return pltpu.make_async_copy(src_ref.at[pl.ds(src_row, 1), :], dst_ref.at[pl.ds(dst_row, 1), :], sem)


def _dispatch_kernel(fill_ref, pos_ref, pos_tail_ref, h_ref, h_tail_ref, xs_ref, zero_ref, sem, fill_sem):
    tm = h_ref.shape[0]
    n_tail = h_tail_ref.shape[0]
    i = pl.program_id(0)
    last = pl.num_programs(0) - 1

    @pl.when(i == 0)
    def _():
        zero_ref[...] = jnp.zeros(zero_ref.shape, F32)
        tile = lambda n: pltpu.make_async_copy(
            zero_ref, xs_ref.at[pl.ds(pl.multiple_of(n * ROW_TILE, ROW_TILE), ROW_TILE), :], fill_sem)

        def start(n, carry):
            @pl.when(fill_ref[n] == 1)
            def _():
                tile(n).start()
            return carry

        def finish(n, carry):
            @pl.when(fill_ref[n] == 1)
            def _():
                tile(n).wait()
            return carry

        lax.fori_loop(0, fill_ref.shape[0], start, 0)
        lax.fori_loop(0, fill_ref.shape[0], finish, 0)

    def scatter(src_ref, positions, count):
        def issue(t, carry):
            for k in range(TOP_K):
                _row_copy(src_ref, t, xs_ref, positions[0, 0, t * TOP_K + k], sem).start()
            return carry

        lax.fori_loop(0, count, issue, 0)

    @pl.when(i < last)
    def _():
        scatter(h_ref, pos_ref, tm)
        for _ in range(TOP_K):
            pltpu.make_async_copy(h_ref, xs_ref.at[pl.ds(0, tm), :], sem).wait()

    @pl.when(i == last)
    def _():
        scatter(h_tail_ref, pos_tail_ref, n_tail)
        for _ in range(TOP_K):
            pltpu.make_async_copy(h_tail_ref, xs_ref.at[pl.ds(0, n_tail), :], sem).wait()


def _dispatch(tile_fill, pos, pos_tail, h, h_tail, n_rows, tm):
    n = h.shape[0]
    steps = n // tm
    n_tail = h_tail.shape[0]
    clamp = lambda i, *_: (jnp.minimum(i, steps - 1), 0, 0)
    grid_spec = pltpu.PrefetchScalarGridSpec(
        num_scalar_prefetch=1, grid=(steps + 1,),
        in_specs=[pl.BlockSpec((1, 1, tm * TOP_K), clamp, memory_space=pltpu.SMEM),
                  pl.BlockSpec((1, 1, n_tail * TOP_K), lambda i, *_: (0, 0, 0), memory_space=pltpu.SMEM),
                  pl.BlockSpec((tm, D_MODEL), lambda i, *_: (jnp.minimum(i, steps - 1), 0)),
                  pl.BlockSpec((n_tail, D_MODEL), lambda i, *_: (0, 0))],
        out_specs=pl.BlockSpec(memory_space=pl.ANY),
        scratch_shapes=[pltpu.VMEM((ROW_TILE, D_MODEL), F32), pltpu.SemaphoreType.DMA(()),
                        pltpu.SemaphoreType.DMA(())])
    return pl.pallas_call(
        _dispatch_kernel, grid_spec=grid_spec,
        out_shape=jax.ShapeDtypeStruct((n_rows, D_MODEL), F32),
        compiler_params=_params("arbitrary"), name="moe_dispatch",
    )(tile_fill, pos.reshape(steps, 1, tm * TOP_K), pos_tail.reshape(1, 1, n_tail * TOP_K), h, h_tail)


def _expert_weight_copies(wgu_hbm, wd_hbm, wgu_buf, wd_buf, sems, expert, slot):
    return (pltpu.make_async_copy(wgu_hbm.at[expert], wgu_buf.at[slot], sems.at[0, slot]),
            pltpu.make_async_copy(wd_hbm.at[expert], wd_buf.at[slot], sems.at[1, slot]))


def _experts_kernel(te_ref, tv_ref, slot_ref, nxt_ref, xs_ref, wgu_hbm, bgu_ref, wd_hbm, bd_ref, o_ref,
                    wgu_buf, wd_buf, wgu_bf, wd_bf, sems):
    n = pl.program_id(0)
    prev = te_ref[jnp.maximum(n - 1, 0)]
    copies = functools.partial(_expert_weight_copies, wgu_hbm, wd_hbm, wgu_buf, wd_buf, sems)

    @pl.when(n == 0)
    def _():
        for c in copies(te_ref[0], slot_ref[0]):
            c.start()

    @pl.when((n == 0) | (te_ref[n] != prev))
    def _():
        slot = slot_ref[n]
        for c in copies(te_ref[n], slot):
            c.wait()

        @pl.when(nxt_ref[n] >= 0)
        def _():
            for c in copies(nxt_ref[n], 1 - slot):
                c.start()

        wgu_bf[...] = wgu_buf[slot].astype(BF16)
        wd_bf[...] = wd_buf[slot].astype(BF16)

    @pl.when(tv_ref[n] == 1)
    def _():
        gu = _dot(xs_ref[...].astype(BF16), wgu_bf[...]) + bgu_ref[0]
        gate = jnp.minimum(gu[:, :D_FF], SWIGLU_LIMIT)
        up = jnp.clip(gu[:, D_FF:], -SWIGLU_LIMIT, SWIGLU_LIMIT)
        act = (up + 1.0) * gate * _sigmoid(SWIGLU_ALPHA * gate)
        o_ref[...] = _dot(act.astype(BF16), wd_bf[...]) + bd_ref[0]

    @pl.when(tv_ref[n] == 0)
    def _():
        o_ref[...] = jnp.zeros(o_ref.shape, F32)


def _experts(tile_expert, tile_valid, tile_slot, next_expert, xs, w_gate_up, b_gate_up, w_down, b_down):
    rows = xs.shape[0]
    tm = ROW_TILE
    e3 = lambda n, te, *_: (te[n], 0, 0)
    row = lambda n, *_: (n, 0)
    grid_spec = pltpu.PrefetchScalarGridSpec(
        num_scalar_prefetch=4, grid=(rows // tm,),
        in_specs=[pl.BlockSpec((tm, D_MODEL), row),
                  pl.BlockSpec(memory_space=pl.ANY),
                  pl.BlockSpec((1, 1, 2 * D_FF), e3),
                  pl.BlockSpec(memory_space=pl.ANY),
                  pl.BlockSpec((1, 1, D_MODEL), e3)],
        out_specs=pl.BlockSpec((tm, D_MODEL), row),
        scratch_shapes=[pltpu.VMEM((2, D_MODEL, 2 * D_FF), F32), pltpu.VMEM((2, D_FF, D_MODEL), F32),
                        pltpu.VMEM((D_MODEL, 2 * D_FF), BF16), pltpu.VMEM((D_FF, D_MODEL), BF16),
                        pltpu.SemaphoreType.DMA((2, 2))])
    return pl.pallas_call(_experts_kernel, grid_spec=grid_spec,
                          out_shape=jax.ShapeDtypeStruct((rows, D_MODEL), F32),
                          compiler_params=_params("arbitrary"), name="moe_experts",
                          )(tile_expert, tile_valid, tile_slot, next_expert, xs, w_gate_up,
                            b_gate_up.reshape(N_EXPERTS, 1, 2 * D_FF), w_down,
                            b_down.reshape(N_EXPERTS, 1, D_MODEL))


def _combine_kernel(pos_ref, pos_next_ref, wts_ref, h_ref, ys_ref, g_ref, b_ref, o_ref, buf_ref, sems):
    tm = h_ref.shape[0]
    i = pl.program_id(0)
    slot = i % 2

    def gather(positions, dst_slot):
        def issue(t, carry):
            for k in range(TOP_K):
                _row_copy(ys_ref, positions[0, 0, t * TOP_K + k], buf_ref.at[dst_slot, k], t,
                          sems.at[dst_slot]).start()
            return carry

        lax.fori_loop(0, tm, issue, 0)

    @pl.when(i == 0)
    def _():
        gather(pos_ref, slot)

    @pl.when(i + 1 < pl.num_programs(0))
    def _():
        gather(pos_next_ref, 1 - slot)

    for k in range(TOP_K):
        pltpu.make_async_copy(ys_ref.at[pl.ds(0, tm), :], buf_ref.at[slot, k], sems.at[slot]).wait()
    wts = wts_ref[...]
    moe = wts[:, 0:1] * buf_ref[slot, 0]
    for k in range(1, TOP_K):
        moe = moe + wts[:, k:k + 1] * buf_ref[slot, k]
    o_ref[...] = _layer_norm(DEEPNORM_ALPHA * h_ref[...] + moe, g_ref[...], b_ref[...])


def _combine(pos, wts, h, ys, g2, b2, tm):
    n = h.shape[0]
    pos3 = pos.reshape(n // tm, 1, tm * TOP_K)
    steps = n // tm
    row = lambda i: (i, 0)
    const = lambda a: pl.BlockSpec(a.shape, lambda i: (0,) * a.ndim)
    pos_block = lambda index: pl.BlockSpec((1, 1, tm * TOP_K), index, memory_space=pltpu.SMEM)
    return pl.pallas_call(
        _combine_kernel, grid=(steps,),
        in_specs=[pos_block(lambda i: (i, 0, 0)),
                  pos_block(lambda i: (jnp.minimum(i + 1, steps - 1), 0, 0)),
                  pl.BlockSpec((tm, TOP_K), row),
                  pl.BlockSpec((tm, D_MODEL), row),
                  pl.BlockSpec(memory_space=pl.ANY), const(g2), const(b2)],
        out_specs=pl.BlockSpec((tm, D_MODEL), row),
        out_shape=jax.ShapeDtypeStruct((n, D_MODEL), F32),
        scratch_shapes=[pltpu.VMEM((2, TOP_K, tm, D_MODEL), F32), pltpu.SemaphoreType.DMA((2,))],
        compiler_params=_params("arbitrary"), name="moe_combine")(pos3, pos3, wts, h, ys, g2, b2)


def _split_in_proj(w_in, b_forget):
    a = N_HEADS * HEAD_DIM
    cuts = [0, a, 2 * a, 3 * a]
    f0 = 3 * a
    rest = f0 + N_HEADS
    starts = cuts[:3] + [rest + i * D_MODEL for i in range(4)]
    w_main = jnp.stack([w_in[:, s:s + D_MODEL] for s in starts]).astype(BF16)
    w_t = jnp.stack([w_in[:, s:s + a].T for s in cuts[1:3]]).astype(BF16)
    w_f = w_in[:, f0:rest].astype(BF16)
    return w_main, w_t, w_f, w_f.T, b_forget.reshape(1, N_HEADS), b_forget.reshape(N_HEADS, 1)


def _routing_tables(counts, n_tiles):
    cnt = counts.reshape(N_EXPERTS).astype(jnp.int32)
    tiles = (cnt + ROW_TILE - 1) // ROW_TILE
    tile_end = jnp.cumsum(tiles)
    start_row = (tile_end - tiles) * ROW_TILE
    n = jnp.arange(n_tiles, dtype=jnp.int32)
    valid = n < tile_end[-1]
    owner = jnp.sum((n[:, None] >= tile_end[None, :]).astype(jnp.int32), axis=1)
    last_owner = jnp.sum((tile_end[-1] - 1 >= tile_end).astype(jnp.int32))
    tile_expert = jnp.where(valid, owner, last_owner).astype(jnp.int32)
    experts = jnp.arange(N_EXPERTS, dtype=jnp.int32)[None, :]
    later = (tiles[None, :] > 0) & (experts > tile_expert[:, None])
    earlier = (tiles[None, :] > 0) & (experts < tile_expert[:, None])
    next_expert = jnp.min(jnp.where(later, experts, N_EXPERTS), axis=1)
    next_expert = jnp.where(next_expert == N_EXPERTS, -1, next_expert).astype(jnp.int32)
    tile_slot = (jnp.sum(earlier.astype(jnp.int32), axis=1) % 2).astype(jnp.int32)
    ends_group = jnp.any((n[:, None] + 1 == tile_end[None, :]) & (tiles[None, :] > 0), axis=1)
    tile_fill = (ends_group | ~valid).astype(jnp.int32)
    return start_row, tile_expert, valid.astype(jnp.int32), tile_slot, next_expert, tile_fill


def _layer(xp, xs, cache_k, cache_v, cache_logf, state_conv, page_table,
           w_in, b_forget, conv_w, conv_b, conv_norm_g, conv_norm_b,
           w_attn_proj, w_conv_proj, w_out, ln1_g, ln1_b,
           w_router, b_router, w_gate_up, b_gate_up, w_down, b_down, ln2_g, ln2_b):
    batch, seq, d = xp.shape
    n_req = xs.shape[0]
    n_p = batch * seq
    row = lambda a: a.reshape(1, -1)
    heads = lambda a: a.reshape(a.shape[0], N_HEADS, HEAD_DIM)

    w_main, w_t, w_f, w_ft, b_f, b_ft = _split_in_proj(w_in, b_forget)
    wa, wc, wo = (w.astype(BF16) for w in (w_attn_proj, w_conv_proj, w_out))
    wr3 = jnp.concatenate(_split3(w_router), axis=1)
    cw, cb, cg, cbn = conv_w, row(conv_b), row(conv_norm_g), row(conv_norm_b)
    g1, b1, g2, b2, br = row(ln1_g), row(ln1_b), row(ln2_g), row(ln2_b), row(b_router)

    xp2 = xp.reshape(n_p, d)
    qb, u_p, ga_p, gc_p, kt_p, vt_p, lft_p, ke, ve, conv_p = _in_projection(
        xp2, w_main, w_f, w_ft, b_f, b_ft, TOKEN_TILE, w_t, (cw, cb, cg, cbn), seq)
    att_p = _prompt_attention(qb, ke, ve, batch, seq)

    xs2 = xs.reshape(n_req, d)
    q_s, u_s, ga_s, gc_s, k_s, v_s, lf_s, lft_s = _in_projection(
        xs2, w_main, w_f, w_ft, b_f, b_ft, n_req)
    att_s = _sample_attention(heads(q_s), heads(k_s), heads(v_s), lft_s,
                              jnp.transpose(cache_k, (0, 2, 3, 1)), jnp.transpose(cache_v, (0, 2, 3, 1)),
                              jnp.transpose(cache_logf, (0, 2, 1)), page_table)
    state_t = jnp.transpose(state_conv, (1, 0, 2))
    conv_s = _conv_sample(state_t, u_s, cw, cb, cg, cbn)

    zero_cnt = jnp.zeros((1, N_EXPERTS), F32)
    h_p, idx_p, wts_p, rank_p, cnt_p = _merge_router(
        att_p, conv_p, ga_p, gc_p, xp2, wa, wc, wo, g1, b1, wr3, br, zero_cnt, TOKEN_TILE)
    h_s, idx_s, wts_s, rank_s, cnt = _merge_router(
        att_s, conv_s, ga_s, gc_s, xs2, wa, wc, wo, g1, b1, wr3, br, cnt_p, n_req)

    n_tok = n_p + n_req
    n_tiles = (n_tok * TOP_K + N_EXPERTS * (ROW_TILE - 1) + ROW_TILE - 1) // ROW_TILE
    start_row, tile_expert, tile_valid, tile_slot, next_expert, tile_fill = _routing_tables(cnt, n_tiles)
    expert_ids = jnp.arange(N_EXPERTS, dtype=jnp.int32)
    slot_of = lambda idx, rank: rank + jnp.sum(
        jnp.where(idx[..., None] == expert_ids, start_row, 0), axis=-1)
    pos_p = slot_of(idx_p, rank_p)
    pos_s = slot_of(idx_s, rank_s)
    sorted_rows = _dispatch(tile_fill, pos_p, pos_s, h_p, h_s, n_tiles * ROW_TILE, TOKEN_TILE)
    expert_out = _experts(tile_expert, tile_valid, tile_slot, next_expert, sorted_rows,
                          w_gate_up, b_gate_up, w_down, b_down)
    y_p = _combine(pos_p, wts_p, h_p, expert_out, g2, b2, TOKEN_TILE)
    y_s = _combine(pos_s, wts_s, h_s, expert_out, g2, b2, n_req)

    hist = CONV_WIDTH - 1
    token_major = lambda t: jnp.transpose(t.reshape(batch, N_HEADS, HEAD_DIM, seq), (0, 3, 1, 2))
    conv_state_p = u_p.reshape(batch, seq, d)[:, seq - hist:, :]
    conv_state_s = jnp.transpose(jnp.concatenate([state_t[1:], u_s[None]], axis=0), (1, 0, 2))
    return (y_p.reshape(batch, seq, d), y_s.reshape(n_req, 1, d),
            token_major(kt_p), token_major(vt_p), jnp.transpose(lft_p, (0, 2, 1)), conv_state_p,
            k_s.reshape(n_req, 1, N_HEADS, HEAD_DIM), v_s.reshape(n_req, 1, N_HEADS, HEAD_DIM),
            lf_s.reshape(n_req, 1, N_HEADS), conv_state_s)


def kernel(x_prompt, x_sample, cache_k, cache_v, cache_logf, state_conv, page_table, w_in, b_forget, conv_w, conv_b, conv_norm_g, conv_norm_b, w_attn_proj, w_conv_proj, w_out, ln1_g, ln1_b, w_router, b_router, w_gate_up, b_gate_up, w_down, b_down, ln2_g, ln2_b):
    assert x_prompt.shape[-1] == D_MODEL and w_in.shape[0] == DEPTH
    out = _layer(x_prompt, x_sample, cache_k[0], cache_v[0], cache_logf[0], state_conv[0], page_table,
                 w_in[0], b_forget[0], conv_w[0], conv_b[0], conv_norm_g[0], conv_norm_b[0],
                 w_attn_proj[0], w_conv_proj[0], w_out[0], ln1_g[0], ln1_b[0],
                 w_router[0], b_router[0], w_gate_up[0], b_gate_up[0], w_down[0], b_down[0],
                 ln2_g[0], ln2_b[0])
    y_p, y_s = out[0], out[1]
    return (y_p, y_s) + tuple(o[None] for o in out[2:])
```

```python
import functools
import math

import jax
import jax.numpy as jnp
from jax import lax
from jax.experimental import pallas as pl
from jax.experimental.pallas import tpu as pltpu

F32 = jnp.float32
BF16 = jnp.bfloat16

D_MODEL = 1024
N_HEADS = 16
HEAD_DIM = 64
CONV_WIDTH = 31
N_EXPERTS = 32
TOP_K = 4
D_FF = 1024
PAGE_SIZE = 128
SWIGLU_ALPHA = 1.702
SWIGLU_LIMIT = 7.0
LN_EPS = 1e-5
DEPTH = 1
DEEPNORM_ALPHA = (2 * DEPTH) ** 0.25
LOG2E = math.log2(math.e)

LANES = 128
SUBLANES = 8
HEADS_PER_LANE_TILE = LANES // HEAD_DIM
VMEM_LIMIT = 56 * 2 ** 20

TOKEN_TILE = 256
ATTN_TILE = 512
ROW_TILE = 256
PAGES_PER_STEP = 16
N_BIAS = 3
KEY_EXT = N_HEADS * LANES


def _params(*sem):
    return pltpu.CompilerParams(dimension_semantics=sem, vmem_limit_bytes=VMEM_LIMIT)


def _split3(x):
    hi = x.astype(BF16)
    r = x - hi.astype(F32)
    mid = r.astype(BF16)
    lo = (r - mid.astype(F32)).astype(BF16)
    return hi, mid, lo


def _dot(a, b):
    return jnp.dot(a, b, preferred_element_type=F32)


def _cumsum_lanes(x, upper, carry):
    hi, mid, lo = _split3(x)
    return _dot(hi, upper) + _dot(mid, upper) + _dot(lo, upper) + carry


def _upper_ones(n):
    rr = lax.broadcasted_iota(jnp.int32, (n, n), 0)
    cc = lax.broadcasted_iota(jnp.int32, (n, n), 1)
    return jnp.where(rr <= cc, 1.0, 0.0).astype(BF16)


def _log_sigmoid(x):
    return jnp.minimum(x, 0.0) - jnp.log1p(jnp.exp(-jnp.abs(x)))


def _sigmoid(x):
    return 1.0 / (1.0 + jnp.exp(-x))


def _layer_norm(x, g, b):
    mu = jnp.mean(x, axis=-1, keepdims=True)
    xc = x - mu
    var = jnp.mean(xc * xc, axis=-1, keepdims=True)
    return xc * lax.rsqrt(var + LN_EPS) * g + b


def _tree(xs, op):
    xs = list(xs)
    while len(xs) > 1:
        xs = [op(xs[i], xs[i + 1]) if i + 1 < len(xs) else xs[i] for i in range(0, len(xs), 2)]
    return xs[0]


def _extended_rows(rows, extra, head):
    pad = jnp.zeros((HEAD_DIM - SUBLANES, rows.shape[1]), F32)
    return [rows, extra, pad] if head % 2 == 0 else [extra, pad, rows]


CONV_HALO = 32


def _conv_tile(u, buf_ref, y_ref, w_ref, b_ref, g_ref, bn_ref):
    ts = u.shape[0]
    halo = CONV_HALO
    buf_ref[0:halo, :] = buf_ref[ts:ts + halo, :]
    buf_ref[halo:halo + ts, :] = u
    first_off = halo - (CONV_WIDTH - 1)
    for c in range(D_MODEL // LANES):
        sl = slice(c * LANES, (c + 1) * LANES)
        y = None
        for r in range(SUBLANES):
            rows = ts if r == 0 else ts + SUBLANES
            z = None
            for a in range((halo + SUBLANES) // SUBLANES):
                tap = a * SUBLANES + r - first_off
                if 0 <= tap < CONV_WIDTH:
                    term = buf_ref[a * SUBLANES:a * SUBLANES + rows, sl] * w_ref[tap:tap + 1, sl]
                    z = term if z is None else z + term
            part = z if r == 0 else z[r:r + ts]
            y = part if y is None else y + part
        y_ref[:, sl] = y + b_ref[:, sl]
    y = _layer_norm(y_ref[...], g_ref[...], bn_ref[...])
    return (y * _sigmoid(y)).astype(BF16)


def _inproj_kernel(*refs, emit_keys, tiles_per_seq):
    if emit_keys:
        (x_ref, w_ref, wf_ref, wft_ref, bf_ref, bft_ref, wt_ref, cw_ref, cb_ref, cg_ref, cbn_ref,
         q_ref, u_ref, ga_ref, gc_ref, kt_ref, vt_ref, lft_ref, ke_ref, ve_ref, conv_ref,
         carry_ref, buf_ref, y_ref) = refs
    else:
        (x_ref, w_ref, wf_ref, wft_ref, bf_ref, bft_ref,
         q_ref, u_ref, ga_ref, gc_ref, k_ref, v_ref, lf_ref, lft_ref) = refs
    tm = x_ref.shape[0]
    nt = (((1,), (1,)), ((), ()))
    x = x_ref[...].astype(BF16)
    q = _dot(x, w_ref[0])
    q_ref[...] = (q * (LOG2E * HEAD_DIM ** -0.5)).astype(BF16)
    u = _dot(x, w_ref[3]) * _sigmoid(_dot(x, w_ref[4]))
    u_ref[...] = u
    if emit_keys:
        @pl.when(pl.program_id(0) % tiles_per_seq == 0)
        def _():
            carry_ref[...] = jnp.zeros(carry_ref.shape, F32)
            buf_ref[tm:tm + CONV_HALO, :] = jnp.zeros((CONV_HALO, D_MODEL), F32)

        conv_ref[...] = _conv_tile(u, buf_ref, y_ref, cw_ref, cb_ref, cg_ref, cbn_ref)
    ga_ref[...] = _sigmoid(_dot(x, w_ref[5]))
    gc_ref[...] = _sigmoid(_dot(x, w_ref[6]))
    ft = lax.dot_general(wft_ref[...], x, nt, preferred_element_type=F32)
    lft = _log_sigmoid(ft + bft_ref[...])
    if not emit_keys:
        k_ref[...] = _dot(x, w_ref[1])
        v_ref[...] = _dot(x, w_ref[2])
        lf_ref[...] = _log_sigmoid(_dot(x, wf_ref[...]) + bf_ref[...])
        lft_ref[...] = lft
        return

    kt = lax.dot_general(wt_ref[0], x, nt, preferred_element_type=F32)
    vt = lax.dot_general(wt_ref[1], x, nt, preferred_element_type=F32)
    kt_ref[0] = kt
    vt_ref[0] = vt
    lft_ref[0] = lft

    csum = _cumsum_lanes(lft * LOG2E, _upper_ones(tm), carry_ref[...])
    carry_ref[...] = csum[:, tm - 1:tm]
    pieces = [p.astype(F32) for p in _split3(-csum)]
    sub = lax.broadcasted_iota(jnp.int32, (SUBLANES, tm), 0)
    ones_rows = jnp.where(sub == 0, 1.0, 0.0)
    k_blocks, v_blocks = [], []
    for h in range(N_HEADS):
        bias_rows = jnp.zeros((SUBLANES, tm), F32)
        for i, piece in enumerate(pieces):
            bias_rows = jnp.where(sub == i, piece[h:h + 1, :], bias_rows)
        feat = slice(h * HEAD_DIM, (h + 1) * HEAD_DIM)
        k_blocks += _extended_rows(kt[feat], bias_rows, h)
        v_blocks += _extended_rows(vt[feat], ones_rows, h)
    ke_ref[0] = jnp.concatenate(k_blocks, axis=0).astype(BF16)
    ve_ref[0] = jnp.concatenate(v_blocks, axis=0).astype(BF16)


def _in_projection(x, w_main, w_f, w_ft, b_f, b_ft, tm, w_t=None, conv=None, seq=None):
    emit_keys = seq is not None
    n = x.shape[0]
    row = lambda i: (i, 0)
    const = lambda a: pl.BlockSpec(a.shape, lambda i: (0,) * a.ndim)
    once = lambda a: pl.BlockSpec(a.shape, lambda i: (0,) * a.ndim, pipeline_mode=pl.Buffered(1))
    big = lambda dt: jax.ShapeDtypeStruct((n, D_MODEL), dt)
    tile = pl.BlockSpec((tm, D_MODEL), row)
    args = [x, w_main, w_f, w_ft, b_f, b_ft]
    in_specs = [tile, once(w_main), const(w_f), const(w_ft), const(b_f), const(b_ft)]
    out_shape = [big(BF16), big(F32), big(F32), big(F32)]
    out_specs = [tile, tile, tile, tile]
    scratch = []
    if emit_keys:
        tiles_per_seq = seq // tm
        batch = n // seq
        seq_major = lambda rows, dt: jax.ShapeDtypeStruct((batch, rows, seq), dt)
        seq_tile = lambda rows: pl.BlockSpec(
            (1, rows, tm), lambda i: (i // tiles_per_seq, 0, i % tiles_per_seq))
        args += [w_t, *conv]
        in_specs += [once(w_t)] + [const(a) for a in conv]
        out_shape += [seq_major(D_MODEL, F32), seq_major(D_MODEL, F32), seq_major(N_HEADS, F32),
                      seq_major(KEY_EXT, BF16), seq_major(KEY_EXT, BF16), big(BF16)]
        out_specs += [seq_tile(D_MODEL), seq_tile(D_MODEL), seq_tile(N_HEADS),
                      seq_tile(KEY_EXT), seq_tile(KEY_EXT), tile]
        scratch = [pltpu.VMEM((N_HEADS, 1), F32), pltpu.VMEM((CONV_HALO + tm, D_MODEL), F32),
                   pltpu.VMEM((tm, D_MODEL), F32)]
    else:
        tiles_per_seq = 1
        out_shape += [big(F32), big(F32), jax.ShapeDtypeStruct((n, N_HEADS), F32),
                      jax.ShapeDtypeStruct((N_HEADS, n), F32)]
        out_specs += [tile, tile, pl.BlockSpec((tm, N_HEADS), row),
                      pl.BlockSpec((N_HEADS, tm), lambda i: (0, i))]
    body = functools.partial(_inproj_kernel, emit_keys=emit_keys, tiles_per_seq=tiles_per_seq)
    return pl.pallas_call(body, grid=(n // tm,), in_specs=in_specs, out_specs=out_specs,
                          out_shape=out_shape, scratch_shapes=scratch,
                          compiler_params=_params("arbitrary"), name="in_projection")(*args)


def _flash_kernel(q_ref, ke_ref, ve_ref, o_ref):
    t = ATTN_TILE
    i = pl.program_id(2)
    lane = lax.broadcasted_iota(jnp.int32, (1, LANES), 1)
    first = lane < HEAD_DIM
    ones_at = lambda cond: jnp.where(cond, 1.0, 0.0).astype(BF16)
    q2 = q_ref[...]
    q_ext = (jnp.where(first, q2, ones_at(lane < HEAD_DIM + N_BIAS)),
             jnp.where(first, ones_at(lane < N_BIAS), q2))
    den_lane = (HEAD_DIM, 0)
    nt = (((1,), (1,)), ((), ()))

    def step(j, carry, masked):
        ks = pl.multiple_of(j * t, t)
        out = []
        for h in range(HEADS_PER_LANE_TILE):
            m, acc = carry[2 * h], carry[2 * h + 1]
            rows = slice(h * LANES, (h + 1) * LANES)
            s = _dot(q_ext[h], ke_ref[0, rows, pl.ds(ks, t)])
            if masked:
                rr = lax.broadcasted_iota(jnp.int32, (t, t), 0)
                cc = lax.broadcasted_iota(jnp.int32, (t, t), 1)
                s = jnp.where(cc <= rr, s, -jnp.inf)
            m_new = jnp.maximum(m, jnp.max(s, axis=-1, keepdims=True))
            alpha = jnp.exp2(m - m_new)
            p = jnp.exp2(s - m_new).astype(BF16)
            pv = lax.dot_general(p, ve_ref[0, rows, pl.ds(ks, t)], nt, preferred_element_type=F32)
            out += [m_new, alpha * acc + pv]
        return tuple(out)

    neg = jnp.full((t, 1), -jnp.inf, F32)
    za = jnp.zeros((t, LANES), F32)
    carry = lax.fori_loop(0, i, lambda j, c: step(j, c, False), (neg, za, neg, za))
    _, acc0, _, acc1 = step(i, carry, True)
    inv0 = 1.0 / acc0[:, den_lane[0]:den_lane[0] + 1]
    inv1 = 1.0 / acc1[:, den_lane[1]:den_lane[1] + 1]
    o_ref[...] = jnp.where(first, acc0 * inv0, acc1 * inv1).astype(o_ref.dtype)


def _prompt_attention(qb, ke, ve, batch, seq):
    t = ATTN_TILE
    nq = seq // t
    pairs = N_HEADS // HEADS_PER_LANE_TILE
    q_spec = pl.BlockSpec((t, LANES), lambda b, hp, i: (b * nq + i, hp))
    e_spec = pl.BlockSpec((1, HEADS_PER_LANE_TILE * LANES, seq), lambda b, hp, i: (b, hp, 0))
    return pl.pallas_call(_flash_kernel, grid=(batch, pairs, nq),
                          in_specs=[q_spec, e_spec, e_spec], out_specs=q_spec,
                          out_shape=jax.ShapeDtypeStruct(qb.shape, BF16),
                          compiler_params=_params("parallel", "parallel", "arbitrary"),
                          name="prompt_attention")(qb, ke, ve)


def _paged_kernel(pt_ref, *refs):
    pp = PAGES_PER_STEP
    k_refs, v_refs, lf_refs = refs[:pp], refs[pp:2 * pp], refs[2 * pp:3 * pp]
    qrep_ref, q_ref, kn_ref, vn_ref, lfnt_ref, o_ref, m_ref, l_ref, c_ref, acc_ref = refs[3 * pp:]
    r = pl.program_id(0)
    g = pl.program_id(1)
    wide = (N_HEADS, PAGE_SIZE)

    @pl.when(g == 0)
    def _():
        m_ref[...] = jnp.full(m_ref.shape, -jnp.inf, F32)
        l_ref[...] = jnp.zeros(l_ref.shape, F32)
        c_ref[...] = jnp.zeros(c_ref.shape, F32)
        acc_ref[...] = jnp.zeros(acc_ref.shape, F32)

    upper = _upper_ones(PAGE_SIZE)
    carry = c_ref[...]
    logits = []
    for p_i in range(pp):
        incl = _cumsum_lanes(lf_refs[p_i][0] * LOG2E, upper, carry)
        carry = jnp.broadcast_to(incl[:, PAGE_SIZE - 1:PAGE_SIZE], wide)
        qk = [jnp.sum(k_refs[p_i][0, h] * qrep_ref[0, h], axis=0, keepdims=True) for h in range(N_HEADS)]
        logits.append(jnp.concatenate(qk, axis=0) - incl)
    c_ref[...] = carry
    m_old = m_ref[...]
    step_max = jnp.max(_tree(logits, jnp.maximum), axis=-1, keepdims=True)
    m_new = jnp.maximum(m_old, jnp.broadcast_to(step_max, wide))
    m_ref[...] = m_new
    alpha = jnp.exp2(m_old - m_new)
    probs = [jnp.exp2(lg - m_new) for lg in logits]
    l_ref[...] = alpha * l_ref[...] + _tree(probs, jnp.add)
    for h in range(N_HEADS):
        head = slice(h, h + 1)
        pv = _tree([probs[p_i][head, :] * v_refs[p_i][0, h] for p_i in range(pp)], jnp.add)
        acc_ref[h] = acc_ref[h] * alpha[head, :] + pv

    @pl.when(g == pl.num_programs(1) - 1)
    def _():
        q = q_ref[0].astype(F32)
        s_new = jnp.sum(kn_ref[0] * q, axis=-1, keepdims=True)
        lane = lax.broadcasted_iota(jnp.int32, lfnt_ref.shape, 1)
        lf_new = jnp.sum(jnp.where(lane == r, lfnt_ref[...], 0.0), axis=-1, keepdims=True)
        logit = s_new - (c_ref[:, 0:1] + lf_new * LOG2E)
        m_last = m_ref[:, 0:1]
        m_fin = jnp.maximum(m_last, logit)
        a_fin = jnp.exp2(m_last - m_fin)
        p_new = jnp.exp2(logit - m_fin)
        l_fin = a_fin * jnp.sum(l_ref[...], axis=-1, keepdims=True) + p_new
        for h in range(N_HEADS):
            head = slice(h, h + 1)
            tot = jnp.sum(acc_ref[h], axis=-1, keepdims=True)
            o_ref[0, h] = (a_fin[head] * tot + p_new[head] * vn_ref[0, h]) / l_fin[head]


def _sample_attention(q_s, k_s, v_s, lft_s, cache_kt, cache_vt, cache_lft, page_table):
    n_req, n_pages = page_table.shape
    pp = PAGES_PER_STEP
    steps = n_pages // pp
    q_rep = jnp.broadcast_to(q_s.astype(F32)[..., None], q_s.shape + (PAGE_SIZE,))

    def page_map(p_i, nd):
        return lambda r, g, pt: (pt[r * n_pages + g * pp + p_i],) + (0,) * nd

    k_specs = [pl.BlockSpec((1, N_HEADS, HEAD_DIM, PAGE_SIZE), page_map(p, 3)) for p in range(pp)]
    lf_specs = [pl.BlockSpec((1, N_HEADS, PAGE_SIZE), page_map(p, 2)) for p in range(pp)]
    req3 = lambda r, g, pt: (r, 0, 0)
    req4 = lambda r, g, pt: (r, 0, 0, 0)
    head = pl.BlockSpec((1, N_HEADS, HEAD_DIM), req3)
    column = pl.BlockSpec((1, N_HEADS, HEAD_DIM, 1), req4)
    wide = lambda: pltpu.VMEM((N_HEADS, PAGE_SIZE), F32)
    grid_spec = pltpu.PrefetchScalarGridSpec(
        num_scalar_prefetch=1, grid=(n_req, steps),
        in_specs=k_specs + k_specs + lf_specs
        + [pl.BlockSpec((1, N_HEADS, HEAD_DIM, PAGE_SIZE), req4), head, head, column,
           pl.BlockSpec(lft_s.shape, lambda r, g, pt: (0, 0))],
        out_specs=column,
        scratch_shapes=[wide(), wide(), wide(), pltpu.VMEM((N_HEADS, HEAD_DIM, PAGE_SIZE), F32)])
    args = [cache_kt] * pp + [cache_vt] * pp + [cache_lft] * pp
    args += [q_rep, q_s, k_s, v_s[..., None], lft_s]
    out = pl.pallas_call(_paged_kernel, grid_spec=grid_spec,
                         out_shape=jax.ShapeDtypeStruct((n_req, N_HEADS, HEAD_DIM, 1), F32),
                         compiler_params=_params("arbitrary", "arbitrary"),
                         name="sample_attention")(page_table.reshape(-1), *args)
    return out.reshape(n_req, N_HEADS * HEAD_DIM).astype(BF16)


def _conv_sample_kernel(state_ref, u_ref, w_ref, b_ref, g_ref, bn_ref, o_ref):
    hist = CONV_WIDTH - 1
    y = u_ref[...] * w_ref[hist:hist + 1, :] + b_ref[...]
    for tap in range(hist):
        y = y + state_ref[tap] * w_ref[tap:tap + 1, :]
    y = _layer_norm(y, g_ref[...], bn_ref[...])
    o_ref[...] = (y * _sigmoid(y)).astype(o_ref.dtype)


def _conv_sample(state_t, u, conv_w, conv_b, g, bn):
    full = lambda a: pl.BlockSpec(a.shape, lambda i: (0,) * a.ndim)
    args = (state_t, u, conv_w, conv_b, g, bn)
    return pl.pallas_call(_conv_sample_kernel, grid=(1,), in_specs=[full(a) for a in args],
                          out_specs=full(u), out_shape=jax.ShapeDtypeStruct(u.shape, BF16),
                          compiler_params=_params("arbitrary"), name="conv_sample")(*args)


def _merge_router_kernel(att_ref, conv_ref, ga_ref, gc_ref, x_ref, wa_ref, wc_ref, wo_ref,
                         g1_ref, b1_ref, wr_ref, br_ref, cnt_in_ref,
                         h_ref, idx_ref, wts_ref, rank_ref, cnt_ref):
    tm = x_ref.shape[0]

    @pl.when(pl.program_id(0) == 0)
    def _():
        cnt_ref[...] = cnt_in_ref[...]

    a = _dot(att_ref[...], wa_ref[...])
    c = _dot(conv_ref[...], wc_ref[...])
    mixed = (ga_ref[...] * a + gc_ref[...] * c).astype(BF16)
    res = DEEPNORM_ALPHA * x_ref[...] + _dot(mixed, wo_ref[...])
    h = _layer_norm(res, g1_ref[...], b1_ref[...])
    h_ref[...] = h

    hh, hm, hl = _split3(h)
    e = N_EXPERTS
    by_hi, by_mid, by_lo = _dot(hh, wr_ref[...]), _dot(hm, wr_ref[...]), _dot(hl, wr_ref[...])
    logits = (by_hi[:, 0:e] + by_hi[:, e:2 * e] + by_mid[:, 0:e]
              + by_hi[:, 2 * e:3 * e] + by_mid[:, e:2 * e] + by_lo[:, 0:e]) + br_ref[...]

    eid = lax.broadcasted_iota(jnp.int32, (tm, N_EXPERTS), 1).astype(F32)
    k_lane = lax.broadcasted_iota(jnp.int32, (tm, TOP_K), 1)
    remaining = logits
    chosen = jnp.zeros((tm, N_EXPERTS), F32)
    vals, picks = [], []
    for _ in range(TOP_K):
        mx = jnp.max(remaining, axis=-1, keepdims=True)
        pick = jnp.min(jnp.where(remaining == mx, eid, N_EXPERTS), axis=-1, keepdims=True)
        hit = eid == pick
        chosen = jnp.where(hit, 1.0, chosen)
        remaining = jnp.where(hit, -jnp.inf, remaining)
        vals.append(mx)
        picks.append(pick)
    exps = [jnp.exp(v - vals[0]) for v in vals]
    denom = exps[0] + exps[1] + exps[2] + exps[3]

    rr = lax.broadcasted_iota(jnp.int32, (tm, tm), 0)
    cc = lax.broadcasted_iota(jnp.int32, (tm, tm), 1)
    strict_lower = jnp.where(cc < rr, 1.0, 0.0).astype(BF16)
    rank_dense = _dot(strict_lower, chosen.astype(BF16)) + cnt_ref[...]

    idx_out = jnp.zeros((tm, TOP_K), F32)
    wts_out = jnp.zeros((tm, TOP_K), F32)
    rank_out = jnp.zeros((tm, TOP_K), F32)
    for k in range(TOP_K):
        rk = jnp.sum(jnp.where(eid == picks[k], rank_dense, 0.0), axis=-1, keepdims=True)
        idx_out = jnp.where(k_lane == k, picks[k], idx_out)
        wts_out = jnp.where(k_lane == k, exps[k] / denom, wts_out)
        rank_out = jnp.where(k_lane == k, rk, rank_out)
    idx_ref[...] = idx_out.astype(jnp.int32)
    wts_ref[...] = wts_out
    rank_ref[...] = rank_out.astype(jnp.int32)
    cnt_ref[...] = cnt_ref[...] + jnp.sum(chosen, axis=0, keepdims=True)


def _merge_router(att, conv, ga, gc, x, wa, wc, wo, g1, b1, wr3, br, cnt_in, tm):
    n = x.shape[0]
    row = lambda i: (i, 0)
    tile = pl.BlockSpec((tm, D_MODEL), row)
    small = pl.BlockSpec((tm, TOP_K), row)
    const = lambda a: pl.BlockSpec(a.shape, lambda i: (0,) * a.ndim)
    out_shape = (jax.ShapeDtypeStruct((n, D_MODEL), F32),
                 jax.ShapeDtypeStruct((n, TOP_K), jnp.int32),
                 jax.ShapeDtypeStruct((n, TOP_K), F32),
                 jax.ShapeDtypeStruct((n, TOP_K), jnp.int32),
                 jax.ShapeDtypeStruct((1, N_EXPERTS), F32))
    return pl.pallas_call(
        _merge_router_kernel, grid=(n // tm,),
        in_specs=[tile, tile, tile, tile, tile, const(wa), const(wc), const(wo),
                  const(g1), const(b1), const(wr3), const(br), const(cnt_in)],
        out_specs=(tile, small, small, small, const(cnt_in)), out_shape=out_shape,
        compiler_params=_params("arbitrary"), name="merge_router",
    )(att, conv, ga, gc, x, wa, wc, wo, g1, b1, wr3, br, cnt_in)


def _row_copy(src_ref, src_row, dst_ref, dst_row, sem):
    return pltpu.make_async_copy(src_ref.at[pl.ds(src_row, 1), :], dst_ref.at[pl.ds(dst_row, 1), :], sem)


def _dispatch_kernel(fill_ref, pos_ref, pos_tail_ref, h_ref, h_tail_ref, xs_ref, zero_ref, sem, fill_sem):
    tm = h_ref.shape[0]
    n_tail = h_tail_ref.shape[0]
    i = pl.program_id(0)
    last = pl.num_programs(0) - 1

    @pl.when(i == 0)
    def _():
        zero_ref[...] = jnp.zeros(zero_ref.shape, F32)
        tile = lambda n: pltpu.make_async_copy(
            zero_ref, xs_ref.at[pl.ds(pl.multiple_of(n * ROW_TILE, ROW_TILE), ROW_TILE), :], fill_sem)

        def start(n, carry):
            @pl.when(fill_ref[n] == 1)
            def _():
                tile(n).start()
            return carry

        def finish(n, carry):
            @pl.when(fill_ref[n] == 1)
            def _():
                tile(n).wait()
            return carry

        lax.fori_loop(0, fill_ref.shape[0], start, 0)
        lax.fori_loop(0, fill_ref.shape[0], finish, 0)

    def scatter(src_ref, positions, count):
        def issue(t, carry):
            for k in range(TOP_K):
                _row_copy(src_ref, t, xs_ref, positions[0, 0, t * TOP_K + k], sem).start()
            return carry

        lax.fori_loop(0, count, issue, 0)

    @pl.when(i < last)
    def _():
        scatter(h_ref, pos_ref, tm)
        for _ in range(TOP_K):
            pltpu.make_async_copy(h_ref, xs_ref.at[pl.ds(0, tm), :], sem).wait()

    @pl.when(i == last)
    def _():
        scatter(h_tail_ref, pos_tail_ref, n_tail)
        for _ in range(TOP_K):
            pltpu.make_async_copy(h_tail_ref, xs_ref.at[pl.ds(0, n_tail), :], sem).wait()


def _dispatch(tile_fill, pos, pos_tail, h, h_tail, n_rows, tm):
    n = h.shape[0]
    steps = n // tm
    n_tail = h_tail.shape[0]
    clamp = lambda i, *_: (jnp.minimum(i, steps - 1), 0, 0)
    grid_spec = pltpu.PrefetchScalarGridSpec(
        num_scalar_prefetch=1, grid=(steps + 1,),
        in_specs=[pl.BlockSpec((1, 1, tm * TOP_K), clamp, memory_space=pltpu.SMEM),
                  pl.BlockSpec((1, 1, n_tail * TOP_K), lambda i, *_: (0, 0, 0), memory_space=pltpu.SMEM),
                  pl.BlockSpec((tm, D_MODEL), lambda i, *_: (jnp.minimum(i, steps - 1), 0)),
                  pl.BlockSpec((n_tail, D_MODEL), lambda i, *_: (0, 0))],
        out_specs=pl.BlockSpec(memory_space=pl.ANY),
        scratch_shapes=[pltpu.VMEM((ROW_TILE, D_MODEL), F32), pltpu.SemaphoreType.DMA(()),
                        pltpu.SemaphoreType.DMA(())])
    return pl.pallas_call(
        _dispatch_kernel, grid_spec=grid_spec,
        out_shape=jax.ShapeDtypeStruct((n_rows, D_MODEL), F32),
        compiler_params=_params("arbitrary"), name="moe_dispatch",
    )(tile_fill, pos.reshape(steps, 1, tm * TOP_K), pos_tail.reshape(1, 1, n_tail * TOP_K), h, h_tail)


def _expert_weight_copies(wgu_hbm, wd_hbm, wgu_buf, wd_buf, sems, expert, slot):
    return (pltpu.make_async_copy(wgu_hbm.at[expert], wgu_buf.at[slot], sems.at[0, slot]),
            pltpu.make_async_copy(wd_hbm.at[expert], wd_buf.at[slot], sems.at[1, slot]))


def _experts_kernel(te_ref, tv_ref, slot_ref, nxt_ref, xs_ref, wgu_hbm, bgu_ref, wd_hbm, bd_ref, o_ref,
                    wgu_buf, wd_buf, wgu_bf, wd_bf, sems):
    n = pl.program_id(0)
    prev = te_ref[jnp.maximum(n - 1, 0)]
    copies = functools.partial(_expert_weight_copies, wgu_hbm, wd_hbm, wgu_buf, wd_buf, sems)

    @pl.when(n == 0)
    def _():
        for c in copies(te_ref[0], slot_ref[0]):
            c.start()

    @pl.when((n == 0) | (te_ref[n] != prev))
    def _():
        slot = slot_ref[n]
        for c in copies(te_ref[n], slot):
            c.wait()

        @pl.when(nxt_ref[n] >= 0)
        def _():
            for c in copies(nxt_ref[n], 1 - slot):
                c.start()

        wgu_bf[...] = wgu_buf[slot].astype(BF16)
        wd_bf[...] = wd_buf[slot].astype(BF16)

    @pl.when(tv_ref[n] == 1)
    def _():
        gu = _dot(xs_ref[...].astype(BF16), wgu_bf[...]) + bgu_ref[0]
        gate = jnp.minimum(gu[:, :D_FF], SWIGLU_LIMIT)
        up = jnp.clip(gu[:, D_FF:], -SWIGLU_LIMIT, SWIGLU_LIMIT)
        act = (up + 1.0) * gate * _sigmoid(SWIGLU_ALPHA * gate)
        o_ref[...] = _dot(act.astype(BF16), wd_bf[...]) + bd_ref[0]

    @pl.when(tv_ref[n] == 0)
    def _():
        o_ref[...] = jnp.zeros(o_ref.shape, F32)


def _experts(tile_expert, tile_valid, tile_slot, next_expert, xs, w_gate_up, b_gate_up, w_down, b_down):
    rows = xs.shape[0]
    tm = ROW_TILE
    e3 = lambda n, te, *_: (te[n], 0, 0)
    row = lambda n, *_: (n, 0)
    grid_spec = pltpu.PrefetchScalarGridSpec(
        num_scalar_prefetch=4, grid=(rows // tm,),
        in_specs=[pl.BlockSpec((tm, D_MODEL), row),
                  pl.BlockSpec(memory_space=pl.ANY),
                  pl.BlockSpec((1, 1, 2 * D_FF), e3),
                  pl.BlockSpec(memory_space=pl.ANY),
                  pl.BlockSpec((1, 1, D_MODEL), e3)],
        out_specs=pl.BlockSpec((tm, D_MODEL), row),
        scratch_shapes=[pltpu.VMEM((2, D_MODEL, 2 * D_FF), F32), pltpu.VMEM((2, D_FF, D_MODEL), F32),
                        pltpu.VMEM((D_MODEL, 2 * D_FF), BF16), pltpu.VMEM((D_FF, D_MODEL), BF16),
                        pltpu.SemaphoreType.DMA((2, 2))])
    return pl.pallas_call(_experts_kernel, grid_spec=grid_spec,
                          out_shape=jax.ShapeDtypeStruct((rows, D_MODEL), F32),
                          compiler_params=_params("arbitrary"), name="moe_experts",
                          )(tile_expert, tile_valid, tile_slot, next_expert, xs, w_gate_up,
                            b_gate_up.reshape(N_EXPERTS, 1, 2 * D_FF), w_down,
                            b_down.reshape(N_EXPERTS, 1, D_MODEL))


def _combine_kernel(pos_ref, pos_next_ref, wts_ref, h_ref, ys_ref, g_ref, b_ref, o_ref, buf_ref, sems):
    tm = h_ref.shape[0]
    i = pl.program_id(0)
    slot = i % 2

    def gather(positions, dst_slot):
        def issue(t, carry):
            for k in range(TOP_K):
                _row_copy(ys_ref, positions[0, 0, t * TOP_K + k], buf_ref.at[dst_slot, k], t,
                          sems.at[dst_slot]).start()
            return carry

        lax.fori_loop(0, tm, issue, 0)

    @pl.when(i == 0)
    def _():
        gather(pos_ref, slot)

    @pl.when(i + 1 < pl.num_programs(0))
    def _():
        gather(pos_next_ref, 1 - slot)

    for k in range(TOP_K):
        pltpu.make_async_copy(ys_ref.at[pl.ds(0, tm), :], buf_ref.at[slot, k], sems.at[slot]).wait()
    wts = wts_ref[...]
    moe = wts[:, 0:1] * buf_ref[slot, 0]
    for k in range(1, TOP_K):
        moe = moe + wts[:, k:k + 1] * buf_ref[slot, k]
    o_ref[...] = _layer_norm(DEEPNORM_ALPHA * h_ref[...] + moe, g_ref[...], b_ref[...])


def _combine(pos, wts, h, ys, g2, b2, tm):
    n = h.shape[0]
    pos3 = pos.reshape(n // tm, 1, tm * TOP_K)
    steps = n // tm
    row = lambda i: (i, 0)
    const = lambda a: pl.BlockSpec(a.shape, lambda i: (0,) * a.ndim)
    pos_block = lambda index: pl.BlockSpec((1, 1, tm * TOP_K), index, memory_space=pltpu.SMEM)
    return pl.pallas_call(
        _combine_kernel, grid=(steps,),
        in_specs=[pos_block(lambda i: (i, 0, 0)),
                  pos_block(lambda i: (jnp.minimum(i + 1, steps - 1), 0, 0)),
                  pl.BlockSpec((tm, TOP_K), row),
                  pl.BlockSpec((tm, D_MODEL), row),
                  pl.BlockSpec(memory_space=pl.ANY), const(g2), const(b2)],
        out_specs=pl.BlockSpec((tm, D_MODEL), row),
        out_shape=jax.ShapeDtypeStruct((n, D_MODEL), F32),
        scratch_shapes=[pltpu.VMEM((2, TOP_K, tm, D_MODEL), F32), pltpu.SemaphoreType.DMA((2,))],
        compiler_params=_params("arbitrary"), name="moe_combine")(pos3, pos3, wts, h, ys, g2, b2)


def _split_in_proj(w_in, b_forget):
    a = N_HEADS * HEAD_DIM
    cuts = [0, a, 2 * a, 3 * a]
    f0 = 3 * a
    rest = f0 + N_HEADS
    starts = cuts[:3] + [rest + i * D_MODEL for i in range(4)]
    w_main = jnp.stack([w_in[:, s:s + D_MODEL] for s in starts]).astype(BF16)
    w_t = jnp.stack([w_in[:, s:s + a].T for s in cuts[1:3]]).astype(BF16)
    w_f = w_in[:, f0:rest].astype(BF16)
    return w_main, w_t, w_f, w_f.T, b_forget.reshape(1, N_HEADS), b_forget.reshape(N_HEADS, 1)


def _routing_tables(counts, n_tiles):
    cnt = counts.reshape(N_EXPERTS).astype(jnp.int32)
    tiles = (cnt + ROW_TILE - 1) // ROW_TILE
    tile_end = jnp.cumsum(tiles)
    start_row = (tile_end - tiles) * ROW_TILE
    n = jnp.arange(n_tiles, dtype=jnp.int32)
    valid = n < tile_end[-1]
    owner = jnp.sum((n[:, None] >= tile_end[None, :]).astype(jnp.int32), axis=1)
    last_owner = jnp.sum((tile_end[-1] - 1 >= tile_end).astype(jnp.int32))
    tile_expert = jnp.where(valid, owner, last_owner).astype(jnp.int32)
    experts = jnp.arange(N_EXPERTS, dtype=jnp.int32)[None, :]
    later = (tiles[None, :] > 0) & (experts > tile_expert[:, None])
    earlier = (tiles[None, :] > 0) & (experts < tile_expert[:, None])
    next_expert = jnp.min(jnp.where(later, experts, N_EXPERTS), axis=1)
    next_expert = jnp.where(next_expert == N_EXPERTS, -1, next_expert).astype(jnp.int32)
    tile_slot = (jnp.sum(earlier.astype(jnp.int32), axis=1) % 2).astype(jnp.int32)
    ends_group = jnp.any((n[:, None] + 1 == tile_end[None, :]) & (tiles[None, :] > 0), axis=1)
    tile_fill = (ends_group | ~valid).astype(jnp.int32)
    return start_row, tile_expert, valid.astype(jnp.int32), tile_slot, next_expert, tile_fill


def _layer(xp, xs, cache_k, cache_v, cache_logf, state_conv, page_table,
           w_in, b_forget, conv_w, conv_b, conv_norm_g, conv_norm_b,
           w_attn_proj, w_conv_proj, w_out, ln1_g, ln1_b,
           w_router, b_router, w_gate_up, b_gate_up, w_down, b_down, ln2_g, ln2_b):
    batch, seq, d = xp.shape
    n_req = xs.shape[0]
    n_p = batch * seq
    row = lambda a: a.reshape(1, -1)
    heads = lambda a: a.reshape(a.shape[0], N_HEADS, HEAD_DIM)

    w_main, w_t, w_f, w_ft, b_f, b_ft = _split_in_proj(w_in, b_forget)
    wa, wc, wo = (w.astype(BF16) for w in (w_attn_proj, w_conv_proj, w_out))
    wr3 = jnp.concatenate(_split3(w_router), axis=1)
    cw, cb, cg, cbn = conv_w, row(conv_b), row(conv_norm_g), row(conv_norm_b)
    g1, b1, g2, b2, br = row(ln1_g), row(ln1_b), row(ln2_g), row(ln2_b), row(b_router)

    xp2 = xp.reshape(n_p, d)
    qb, u_p, ga_p, gc_p, kt_p, vt_p, lft_p, ke, ve, conv_p = _in_projection(
        xp2, w_main, w_f, w_ft, b_f, b_ft, TOKEN_TILE, w_t, (cw, cb, cg, cbn), seq)
    att_p = _prompt_attention(qb, ke, ve, batch, seq)

    xs2 = xs.reshape(n_req, d)
    q_s, u_s, ga_s, gc_s, k_s, v_s, lf_s, lft_s = _in_projection(
        xs2, w_main, w_f, w_ft, b_f, b_ft, n_req)
    att_s = _sample_attention(heads(q_s), heads(k_s), heads(v_s), lft_s,
                              jnp.transpose(cache_k, (0, 2, 3, 1)), jnp.transpose(cache_v, (0, 2, 3, 1)),
                              jnp.transpose(cache_logf, (0, 2, 1)), page_table)
    state_t = jnp.transpose(state_conv, (1, 0, 2))
    conv_s = _conv_sample(state_t, u_s, cw, cb, cg, cbn)

    zero_cnt = jnp.zeros((1, N_EXPERTS), F32)
    h_p, idx_p, wts_p, rank_p, cnt_p = _merge_router(
        att_p, conv_p, ga_p, gc_p, xp2, wa, wc, wo, g1, b1, wr3, br, zero_cnt, TOKEN_TILE)
    h_s, idx_s, wts_s, rank_s, cnt = _merge_router(
        att_s, conv_s, ga_s, gc_s, xs2, wa, wc, wo, g1, b1, wr3, br, cnt_p, n_req)

    n_tok = n_p + n_req
    n_tiles = (n_tok * TOP_K + N_EXPERTS * (ROW_TILE - 1) + ROW_TILE - 1) // ROW_TILE
    start_row, tile_expert, tile_valid, tile_slot, next_expert, tile_fill = _routing_tables(cnt, n_tiles)
    expert_ids = jnp.arange(N_EXPERTS, dtype=jnp.int32)
    slot_of = lambda idx, rank: rank + jnp.sum(
        jnp.where(idx[..., None] == expert_ids, start_row, 0), axis=-1)
    pos_p = slot_of(idx_p, rank_p)
    pos_s = slot_of(idx_s, rank_s)
    sorted_rows = _dispatch(tile_fill, pos_p, pos_s, h_p, h_s, n_tiles * ROW_TILE, TOKEN_TILE)
    expert_out = _experts(tile_expert, tile_valid, tile_slot, next_expert, sorted_rows,
                          w_gate_up, b_gate_up, w_down, b_down)
    y_p = _combine(pos_p, wts_p, h_p, expert_out, g2, b2, TOKEN_TILE)
    y_s = _combine(pos_s, wts_s, h_s, expert_out, g2, b2, n_req)

    hist = CONV_WIDTH - 1
    token_major = lambda t: jnp.transpose(t.reshape(batch, N_HEADS, HEAD_DIM, seq), (0, 3, 1, 2))
    conv_state_p = u_p.reshape(batch, seq, d)[:, seq - hist:, :]
    conv_state_s = jnp.transpose(jnp.concatenate([state_t[1:], u_s[None]], axis=0), (1, 0, 2))
    return (y_p.reshape(batch, seq, d), y_s.reshape(n_req, 1, d),
            token_major(kt_p), token_major(vt_p), jnp.transpose(lft_p, (0, 2, 1)), conv_state_p,
            k_s.reshape(n_req, 1, N_HEADS, HEAD_DIM), v_s.reshape(n_req, 1, N_HEADS, HEAD_DIM),
            lf_s.reshape(n_req, 1, N_HEADS), conv_state_s)


def kernel(x_prompt, x_sample, cache_k, cache_v, cache_logf, state_conv, page_table, w_in, b_forget, conv_w, conv_b, conv_norm_g, conv_norm_b, w_attn_proj, w_conv_proj, w_out, ln1_g, ln1_b, w_router, b_router, w_gate_up, b_gate_up, w_down, b_down, ln2_g, ln2_b):
    assert x_prompt.shape[-1] == D_MODEL and w_in.shape[0] == DEPTH
    out = _layer(x_prompt, x_sample, cache_k[0], cache_v[0], cache_logf[0], state_conv[0], page_table,
                 w_in[0], b_forget[0], conv_w[0], conv_b[0], conv_norm_g[0], conv_norm_b[0],
                 w_attn_proj[0], w_conv_proj[0], w_out[0], ln1_g[0], ln1_b[0],
                 w_router[0], b_router[0], w_gate_up[0], b_gate_up[0], w_down[0], b_down[0],
                 ln2_g[0], ln2_b[0])
    y_p, y_s = out[0], out[1]
    return (y_p, y_s) + tuple(o[None] for o in out[2:])
```
